```python
import jax, jax.numpy as jnp
from jax import lax
import numpy as np

D_MODEL = 1024
BATCH = 8
SEQ = 4096
DEPTH = 1

CONV_CH = 1024
CONV_WIDTH = 31
ATTN_GROUPS = ((128, 1), (512, 4), (2048, 16))
HEADS_PER_GROUP = 8
HEAD_DIM = 64
N_ATTN_HEADS = HEADS_PER_GROUP * len(ATTN_GROUPS)
ATTN_WIDTH = N_ATTN_HEADS * HEAD_DIM
SLOT_WIDTH = HEADS_PER_GROUP * HEAD_DIM
N_BRANCH = 2
D_FF = 2816
FFN_CONV_WIDTH = 3
RMS_EPS = 1e-6
LN_EPS = 1e-5
MASK_VALUE = -1e30
SPLIT_SIZES = (CONV_CH, CONV_CH, ATTN_WIDTH, ATTN_WIDTH, ATTN_WIDTH)
IN_WIDTH = sum(SPLIT_SIZES) + N_BRANCH * D_MODEL

kernel_name = "hybrid_conformer_dilated_attn_gated_encoder"


def rms_norm(x, g):
    xf = x.astype(jnp.float32)
    y = xf * lax.rsqrt(jnp.mean(xf * xf, axis=-1, keepdims=True) + RMS_EPS)
    return (y * g.astype(jnp.float32)).astype(x.dtype)


def layer_norm(x, g, b):
    xf = x.astype(jnp.float32)
    mu = jnp.mean(xf, axis=-1, keepdims=True)
    xc = xf - mu
    var = jnp.mean(xc * xc, axis=-1, keepdims=True)
    y = xc * lax.rsqrt(var + LN_EPS) * g.astype(jnp.float32) + b.astype(jnp.float32)
    return y.astype(x.dtype)


def depthwise_conv(x, w, b):
    k = w.shape[0]
    c = x.shape[-1]
    y = lax.conv_general_dilated(
        x, w[:, None, :].astype(x.dtype), window_strides=(1,),
        padding=[(k // 2, k // 2)], dimension_numbers=('NWC', 'WIO', 'NWC'),
        feature_group_count=c)
    return y + b.astype(x.dtype)


def alibi_slopes(n):
    return 2.0 ** (-8.0 * jnp.arange(1, n + 1, dtype=jnp.float32) / n)


def dilated_window_attention(q, k, v, window, dilation, slopes):
    B, S, H, Dh = q.shape
    r = dilation
    half = window // (2 * r)
    bq = half
    L = S // r
    nb = -(-L // bq)
    Lp = nb * bq

    def to_sub(t):
        return t.reshape(B, L, r, H, Dh).transpose(0, 2, 1, 3, 4)

    qb = jnp.pad(to_sub(q), ((0, 0), (0, 0), (0, Lp - L), (0, 0), (0, 0)))
    qb = qb.reshape(B, r, nb, bq, H, Dh)
    pad_kv = ((0, 0), (0, 0), (bq, Lp - L + bq), (0, 0), (0, 0))

    def kv_blocks(t):
        t = jnp.pad(to_sub(t), pad_kv).reshape(B, r, nb + 2, bq, H, Dh)
        return jnp.concatenate([t[:, :, :-2], t[:, :, 1:-1], t[:, :, 2:]], axis=3)

    kb = kv_blocks(k)
    vb = kv_blocks(v)
    scores = jnp.einsum('bcnqhd,bcnkhd->bchnqk', qb, kb).astype(jnp.float32) * (Dh ** -0.5)
    q_idx = jnp.arange(nb)[:, None] * bq + jnp.arange(bq)[None, :]
    k_idx = (jnp.arange(nb)[:, None] - 1) * bq + jnp.arange(3 * bq)[None, :]
    rel = k_idx[:, None, :] - q_idx[:, :, None]
    valid = (jnp.abs(rel) <= half) & (k_idx[:, None, :] >= 0) & (k_idx[:, None, :] < L)
    dist = (jnp.abs(rel) * r).astype(jnp.float32)
    scores = scores - slopes.astype(jnp.float32)[:, None, None, None] * dist
    scores = jnp.where(valid, scores, MASK_VALUE)
    lse = jax.nn.logsumexp(scores, axis=-1)
    probs = jnp.exp(scores - lse[..., None])
    out = jnp.einsum('bchnqk,bcnkhd->bcnqhd', probs.astype(v.dtype), vb)
    out = out.reshape(B, r, Lp, H, Dh)[:, :, :L].transpose(0, 2, 1, 3, 4).reshape(B, S, H, Dh)
    lse = lse.reshape(B, r, H, Lp)[..., :L].transpose(0, 3, 1, 2).reshape(B, S, H)
    return out, lse


def setup_inputs(seed: int = 0) -> dict:
    key = jax.random.key(seed)
    ks = jax.random.split(key, 17)
    f32 = jnp.float32

    def nrm(k, shape, scale):
        return jax.random.normal(k, shape, f32) * scale

    return {
        "x": jax.random.normal(ks[0], (BATCH, SEQ, D_MODEL), f32),
        "norm_mix_g": 1.0 + nrm(ks[1], (DEPTH, D_MODEL), 0.05),
        "w_in": nrm(ks[2], (DEPTH, D_MODEL, IN_WIDTH), D_MODEL ** -0.5),
        "b_gate": nrm(ks[3], (DEPTH, N_BRANCH * D_MODEL), 0.01),
        "conv_dw_w": nrm(ks[4], (DEPTH, CONV_WIDTH, CONV_CH), CONV_WIDTH ** -0.5),
        "conv_dw_b": nrm(ks[5], (DEPTH, CONV_CH), 0.01),
        "conv_ln_g": 1.0 + nrm(ks[6], (DEPTH, CONV_CH), 0.05),
        "conv_ln_b": nrm(ks[7], (DEPTH, CONV_CH), 0.01),
        "w_conv_out": nrm(ks[8], (DEPTH, CONV_CH, D_MODEL), CONV_CH ** -0.5),
        "w_attn_out": nrm(ks[9], (DEPTH, SLOT_WIDTH, D_MODEL), SLOT_WIDTH ** -0.5),
        "w_out": nrm(ks[10], (DEPTH, D_MODEL, D_MODEL), D_MODEL ** -0.5),
        "norm_ffn_g": 1.0 + nrm(ks[11], (DEPTH, D_MODEL), 0.05),
        "w_up": nrm(ks[12], (DEPTH, D_MODEL, 2 * D_FF), D_MODEL ** -0.5),
        "ffn_dw_w": nrm(ks[13], (DEPTH, FFN_CONV_WIDTH, 2 * D_FF), FFN_CONV_WIDTH ** -0.5),
        "ffn_dw_b": nrm(ks[14], (DEPTH, 2 * D_FF), 0.01),
        "w_down": nrm(ks[15], (DEPTH, D_FF, D_MODEL), D_FF ** -0.5),
        "norm_final_g": 1.0 + nrm(ks[16], (D_MODEL,), 0.05),
    }


def reference(x, norm_mix_g, w_in, b_gate, conv_dw_w, conv_dw_b, conv_ln_g, conv_ln_b,
              w_conv_out, w_attn_out, w_out, norm_ffn_g, w_up, ffn_dw_w, ffn_dw_b,
              w_down, norm_final_g):
    B, S, D = x.shape
    slopes = alibi_slopes(N_ATTN_HEADS)
    split_idx = list(np.cumsum(SPLIT_SIZES))
    h = x
    for l in range(DEPTH):
        u = rms_norm(h, norm_mix_g[l])
        proj = u @ w_in[l]
        conv_a, conv_gate, q, k, v, gate_logits = jnp.split(proj, split_idx, axis=-1)

        z = conv_a * jax.nn.sigmoid(conv_gate)
        z = depthwise_conv(z, conv_dw_w[l], conv_dw_b[l])
        z = jax.nn.silu(layer_norm(z, conv_ln_g[l], conv_ln_b[l]))
        conv_out = z @ w_conv_out[l]

        q = q.reshape(B, S, N_ATTN_HEADS, HEAD_DIM)
        k = k.reshape(B, S, N_ATTN_HEADS, HEAD_DIM)
        v = v.reshape(B, S, N_ATTN_HEADS, HEAD_DIM)
        outs, lses = [], []
        for g, (window, dilation) in enumerate(ATTN_GROUPS):
            hs = slice(g * HEADS_PER_GROUP, (g + 1) * HEADS_PER_GROUP)
            o, lse = dilated_window_attention(q[:, :, hs], k[:, :, hs], v[:, :, hs],
                                              window, dilation, slopes[hs])
            outs.append(o)
            lses.append(lse)
        mix_w = jax.nn.softmax(jnp.stack(lses, axis=0), axis=0)
        attn = jnp.sum(mix_w[..., None] * jnp.stack(outs, axis=0).astype(jnp.float32), axis=0)
        attn_out = attn.astype(x.dtype).reshape(B, S, SLOT_WIDTH) @ w_attn_out[l]

        gates = jax.nn.sigmoid(gate_logits + b_gate[l]).reshape(B, S, N_BRANCH, D)
        merged = gates[:, :, 0] * conv_out + gates[:, :, 1] * attn_out
        h = h + merged @ w_out[l]

        un = rms_norm(h, norm_ffn_g[l])
        up = depthwise_conv(un @ w_up[l], ffn_dw_w[l], ffn_dw_b[l])
        a, val = jnp.split(up, 2, axis=-1)
        h = h + (jax.nn.silu(a) * val) @ w_down[l]
    return rms_norm(h, norm_final_g)
```

```python
import functools

import jax
import jax.numpy as jnp
import numpy as np
from jax import lax
from jax.experimental import pallas as pl
from jax.experimental.pallas import tpu as pltpu

F32 = jnp.float32
BF16 = jnp.bfloat16

D_MODEL = 1024
CONV_CH = 1024
CONV_WIDTH = 31
ATTN_GROUPS = ((128, 1), (512, 4), (2048, 16))
HEADS_PER_GROUP = 8
HEAD_DIM = 64
N_ATTN_HEADS = HEADS_PER_GROUP * len(ATTN_GROUPS)
ATTN_WIDTH = N_ATTN_HEADS * HEAD_DIM
SLOT_WIDTH = HEADS_PER_GROUP * HEAD_DIM
D_FF = 2816
RMS_EPS = 1e-6
LN_EPS = 1e-5
MASK_VALUE = -1e30

_OFF_A, _OFF_G, _OFF_Q, _OFF_K, _OFF_V, _OFF_GATE, _OFF_END = (
    0, CONV_CH, 2 * CONV_CH, 2 * CONV_CH + ATTN_WIDTH, 2 * CONV_CH + 2 * ATTN_WIDTH,
    2 * CONV_CH + 3 * ATTN_WIDTH, 2 * CONV_CH + 3 * ATTN_WIDTH + 2 * D_MODEL)

HALF_BAND = 64
HALO = 16
CONV_HALF = CONV_WIDTH // 2
VMEM_LIMIT = 56 * 1024 * 1024

_SLOPES = (np.float32(2.0) ** (np.float32(-8.0) * np.arange(1, N_ATTN_HEADS + 1, dtype=np.float32)
                               / np.float32(N_ATTN_HEADS))).astype(np.float64)


def _sigmoid(x):
    return 1.0 / (1.0 + jnp.exp(-x))


def _silu(x):
    return x * _sigmoid(x)


def _rms_scale(xf):
    return lax.rsqrt(jnp.mean(xf * xf, axis=-1, keepdims=True) + RMS_EPS)


def _dot(a, b):
    return jnp.dot(a, b, preferred_element_type=F32)


def _resident(shape):
    return pl.BlockSpec(shape, lambda *_: (0,) * len(shape), pipeline_mode=pl.Buffered(1))


def _in_proj_kernel(x_ref, g_ref, w_ref, bg_ref, z_ref, q_ref, k_ref, v_ref, gate_ref):
    xf = x_ref[...]
    u = (xf * _rms_scale(xf) * g_ref[...]).astype(BF16)

    def proj(lo, hi):
        return _dot(u, w_ref[:, lo:hi])

    z_ref[...] = (proj(_OFF_A, _OFF_G) * _sigmoid(proj(_OFF_G, _OFF_Q))).astype(BF16)
    q_ref[...] = (proj(_OFF_Q, _OFF_K) * (HEAD_DIM ** -0.5)).astype(BF16)
    k_ref[...] = proj(_OFF_K, _OFF_V).astype(BF16)
    v_ref[...] = proj(_OFF_V, _OFF_GATE).astype(BF16)
    gate_ref[...] = _sigmoid(proj(_OFF_GATE, _OFF_END) + bg_ref[...]).astype(BF16)


def _in_proj(x2, g, w_in, b_gate, tm):
    t = x2.shape[0]
    row = lambda width: pl.BlockSpec((tm, width), lambda i: (i, 0))
    return pl.pallas_call(
        _in_proj_kernel,
        grid=(t // tm,),
        in_specs=[row(D_MODEL), _resident((1, D_MODEL)), _resident(w_in.shape),
                  _resident((1, 2 * D_MODEL))],
        out_specs=[row(CONV_CH), row(ATTN_WIDTH), row(ATTN_WIDTH), row(ATTN_WIDTH),
                   row(2 * D_MODEL)],
        out_shape=[jax.ShapeDtypeStruct((t, CONV_CH), BF16),
                   jax.ShapeDtypeStruct((t, ATTN_WIDTH), BF16),
                   jax.ShapeDtypeStruct((t, ATTN_WIDTH), BF16),
                   jax.ShapeDtypeStruct((t, ATTN_WIDTH), BF16),
                   jax.ShapeDtypeStruct((t, 2 * D_MODEL), BF16)],
        compiler_params=pltpu.CompilerParams(
            dimension_semantics=("arbitrary",), vmem_limit_bytes=VMEM_LIMIT),
        name="in_proj",
    )(x2, g, w_in, b_gate)


def _attn_kernel(q_ref, k_ref, v_ref, o_ref, lse_ref, *, bq, wk, length, slope_dist):
    m0 = pl.program_id(2) * bq
    start = jnp.clip(m0 - HALF_BAND, 0, length - wk)
    start = pl.multiple_of(start, HALF_BAND)
    kk = k_ref[pl.ds(start, wk), :]
    vv = v_ref[pl.ds(start, wk), :]
    qpos = m0 + lax.broadcasted_iota(jnp.int32, (bq, wk), 0)
    kpos = start + lax.broadcasted_iota(jnp.int32, (bq, wk), 1)
    dist = jnp.abs(kpos - qpos)
    valid = dist <= HALF_BAND
    distf = dist.astype(F32)
    outs, lses = [], []
    for h in range(HEADS_PER_GROUP):
        cols = slice(h * HEAD_DIM, (h + 1) * HEAD_DIM)
        s = lax.dot_general(q_ref[:, cols], kk[:, cols], (((1,), (1,)), ((), ())),
                            preferred_element_type=F32)
        s = jnp.where(valid, s - slope_dist[h] * distf, MASK_VALUE)
        m = jnp.max(s, axis=-1, keepdims=True)
        p = jnp.exp(s - m)
        l = jnp.sum(p, axis=-1, keepdims=True)
        o = _dot(p.astype(BF16), vv[:, cols]) / l
        outs.append(o.astype(BF16))
        lses.append(jnp.broadcast_to(m + jnp.log(l), (bq, HEAD_DIM)))
    o_ref[...] = jnp.concatenate(outs, axis=-1)
    lse_ref[...] = jnp.concatenate(lses, axis=-1)


def _attention_group(q2, k2, v2, group, batch, seq, bq):
    _, r = ATTN_GROUPS[group]
    length = seq // r
    wk = min(bq + 2 * HALF_BAND, length)
    slope_dist = tuple(float(_SLOPES[group * HEADS_PER_GROUP + h] * r) for h in range(HEADS_PER_GROUP))
    n_slot = ATTN_WIDTH // SLOT_WIDTH
    view = lambda a: a.reshape(batch, length, r * ATTN_WIDTH)
    q_spec = pl.BlockSpec((None, bq, SLOT_WIDTH), lambda b, c, i: (b, i, c * n_slot + group))
    kv_spec = pl.BlockSpec((None, length, SLOT_WIDTH), lambda b, c, i: (b, 0, c * n_slot + group))
    out_spec = pl.BlockSpec((None, bq, SLOT_WIDTH), lambda b, c, i: (b, i, c))
    o, lse = pl.pallas_call(
        functools.partial(_attn_kernel, bq=bq, wk=wk, length=length, slope_dist=slope_dist),
        grid=(batch, r, length // bq),
        in_specs=[q_spec, kv_spec, kv_spec],
        out_specs=[out_spec, out_spec],
        out_shape=[jax.ShapeDtypeStruct((batch, length, r * SLOT_WIDTH), BF16),
                   jax.ShapeDtypeStruct((batch, length, r * SLOT_WIDTH), F32)],
        compiler_params=pltpu.CompilerParams(
            dimension_semantics=("arbitrary", "arbitrary", "arbitrary"),
            vmem_limit_bytes=VMEM_LIMIT),
        name=f"attn_g{group}",
    )(view(q2), view(k2), view(v2))
    return o.reshape(batch * seq, SLOT_WIDTH), lse.reshape(batch * seq, SLOT_WIDTH)


_CONV_ROWS = 64
_LANES = 128


def _mix_kernel(zp_ref, zc_ref, zn_ref, dww_ref, dwb_ref, lng_ref, lnb_ref, wco_ref,
                o0_ref, o1_ref, o2_ref, l0_ref, l1_ref, l2_ref, wao_ref,
                gate_ref, wout_ref, x_ref, gffn_ref,
                h_ref, un_ref, zext_ref, y_ref, *, tm, tiles_per_seq):
    pos = pl.program_id(0) % tiles_per_seq
    keep_prev = (pos != 0).astype(F32)
    keep_next = (pos != tiles_per_seq - 1).astype(F32)
    zext_ref[0:HALO, :] = zp_ref[...].astype(F32) * keep_prev
    zext_ref[HALO:HALO + tm, :] = zc_ref[...].astype(F32)
    zext_ref[HALO + tm:, :] = zn_ref[...].astype(F32) * keep_next

    first = HALO - CONV_HALF
    for cb in range(CONV_CH // _LANES):
        cols = slice(cb * _LANES, (cb + 1) * _LANES)
        taps = [dww_ref[j:j + 1, cols] for j in range(CONV_WIDTH)]
        bias = dwb_ref[:, cols]

        for r0 in range(0, tm, _CONV_ROWS):
            acc = jnp.broadcast_to(bias, (_CONV_ROWS, _LANES))
            for j in range(CONV_WIDTH):
                acc = acc + taps[j] * zext_ref[r0 + first + j:r0 + first + j + _CONV_ROWS, cols]
            y_ref[r0:r0 + _CONV_ROWS, cols] = acc

    y = y_ref[...]
    mu = jnp.mean(y, axis=-1, keepdims=True)
    yc = y - mu
    var = jnp.mean(yc * yc, axis=-1, keepdims=True)
    yn = yc * lax.rsqrt(var + LN_EPS) * lng_ref[...] + lnb_ref[...]
    conv_out = _dot(_silu(yn).astype(BF16), wco_ref[...])

    l0, l1, l2 = l0_ref[...], l1_ref[...], l2_ref[...]
    mx = jnp.maximum(jnp.maximum(l0, l1), l2)
    e0, e1, e2 = jnp.exp(l0 - mx), jnp.exp(l1 - mx), jnp.exp(l2 - mx)
    attn = (e0 * o0_ref[...].astype(F32) + e1 * o1_ref[...].astype(F32)
            + e2 * o2_ref[...].astype(F32)) / (e0 + e1 + e2)
    attn_out = _dot(attn.astype(BF16), wao_ref[...])

    merged = (gate_ref[:, :D_MODEL].astype(F32) * conv_out
              + gate_ref[:, D_MODEL:].astype(F32) * attn_out)
    h = x_ref[...] + _dot(merged.astype(BF16), wout_ref[...])
    h_ref[...] = h
    un_ref[...] = (h * _rms_scale(h) * gffn_ref[...]).astype(BF16)


def _halo_specs(tm, width, n_rows):
    per = tm // HALO
    last = n_rows // HALO - 1
    prev = pl.BlockSpec((HALO, width), lambda i: (jnp.maximum(i * per - 1, 0), 0))
    cur = pl.BlockSpec((tm, width), lambda i: (i, 0))
    nxt = pl.BlockSpec((HALO, width), lambda i: (jnp.minimum((i + 1) * per, last), 0))
    return prev, cur, nxt


def _mix(z, dww, dwb, lng, lnb, wco, os_, ls_, wao, gates, wout, x2, gffn, seq, tm):
    t = x2.shape[0]
    row = lambda width: pl.BlockSpec((tm, width), lambda i: (i, 0))
    zp, zc, zn = _halo_specs(tm, CONV_CH, t)
    vec = _resident((1, D_MODEL))
    return pl.pallas_call(
        functools.partial(_mix_kernel, tm=tm, tiles_per_seq=seq // tm),
        grid=(t // tm,),
        in_specs=[zp, zc, zn, _resident(dww.shape), vec, vec, vec, _resident(wco.shape),
                  row(SLOT_WIDTH), row(SLOT_WIDTH), row(SLOT_WIDTH),
                  row(SLOT_WIDTH), row(SLOT_WIDTH), row(SLOT_WIDTH), _resident(wao.shape),
                  row(2 * D_MODEL), _resident(wout.shape), row(D_MODEL), vec],
        out_specs=[row(D_MODEL), row(D_MODEL)],
        out_shape=[jax.ShapeDtypeStruct((t, D_MODEL), F32),
                   jax.ShapeDtypeStruct((t, D_MODEL), BF16)],
        scratch_shapes=[pltpu.VMEM((tm + 2 * HALO, CONV_CH), F32),
                        pltpu.VMEM((tm, CONV_CH), F32)],
        compiler_params=pltpu.CompilerParams(
            dimension_semantics=("arbitrary",), vmem_limit_bytes=VMEM_LIMIT),
        name="mix",
    )(z, z, z, dww, dwb, lng, lnb, wco, *os_, *ls_, wao, gates, wout, x2, gffn)


_FF_CHUNK = 256


def _ffn_kernel(up_ref_prev, uc_ref, un_ref_next, h_ref, wup_ref, dww_ref, dwb_ref, wdn_ref,
                gfin_ref, y_ref, uext_ref, a_ref, b_ref, acc_ref, *, tm, tiles_per_seq):
    pos = pl.program_id(0) % tiles_per_seq
    uext_ref[0:HALO, :] = jnp.where(pos != 0, up_ref_prev[...], jnp.zeros_like(up_ref_prev))
    uext_ref[HALO:HALO + tm, :] = uc_ref[...]
    uext_ref[HALO + tm:, :] = jnp.where(pos != tiles_per_seq - 1, un_ref_next[...],
                                        jnp.zeros_like(un_ref_next))
    uext = uext_ref[...]

    def conv3(src_ref, col0):
        cols = slice(col0, col0 + _FF_CHUNK)
        return (dww_ref[0:1, cols] * src_ref[HALO - 1:HALO - 1 + tm, :]
                + dww_ref[1:2, cols] * src_ref[HALO:HALO + tm, :]
                + dww_ref[2:3, cols] * src_ref[HALO + 1:HALO + 1 + tm, :]
                + dwb_ref[:, cols])

    for c in range(D_FF // _FF_CHUNK):
        ca, cv = c * _FF_CHUNK, D_FF + c * _FF_CHUNK
        a_ref[...] = _dot(uext, wup_ref[:, ca:ca + _FF_CHUNK])
        b_ref[...] = _dot(uext, wup_ref[:, cv:cv + _FF_CHUNK])
        s = (_silu(conv3(a_ref, ca)) * conv3(b_ref, cv)).astype(BF16)
        part = _dot(s, wdn_ref[ca:ca + _FF_CHUNK, :])
        if c == 0:
            acc_ref[...] = part
        else:
            acc_ref[...] += part

    h = h_ref[...] + acc_ref[...]
    y_ref[...] = h * _rms_scale(h) * gfin_ref[...]


def _ffn(un, h, wup, dww, dwb, wdn, gfin, seq, tm):
    t = h.shape[0]
    row = lambda width: pl.BlockSpec((tm, width), lambda i: (i, 0))
    up, uc, unx = _halo_specs(tm, D_MODEL, t)
    return pl.pallas_call(
        functools.partial(_ffn_kernel, tm=tm, tiles_per_seq=seq // tm),
        grid=(t // tm,),
        in_specs=[up, uc, unx, row(D_MODEL), _resident(wup.shape), _resident(dww.shape),
                  _resident(dwb.shape), _resident(wdn.shape), _resident((1, D_MODEL))],
        out_specs=row(D_MODEL),
        out_shape=jax.ShapeDtypeStruct((t, D_MODEL), F32),
        scratch_shapes=[pltpu.VMEM((tm + 2 * HALO, D_MODEL), BF16),
                        pltpu.VMEM((tm + 2 * HALO, _FF_CHUNK), F32),
                        pltpu.VMEM((tm + 2 * HALO, _FF_CHUNK), F32),
                        pltpu.VMEM((tm, D_MODEL), F32)],
        compiler_params=pltpu.CompilerParams(
            dimension_semantics=("arbitrary",), vmem_limit_bytes=VMEM_LIMIT),
        name="ffn",
    )(un, un, un, h, wup, dww, dwb, wdn, gfin)


def kernel(x, norm_mix_g, w_in, b_gate, conv_dw_w, conv_dw_b, conv_ln_g, conv_ln_b, w_conv_out,
           w_attn_out, w_out, norm_ffn_g, w_up, ffn_dw_w, ffn_dw_b, w_down, norm_final_g):
    batch, seq, d = x.shape
    depth = w_in.shape[0]
    h2 = x.reshape(batch * seq, d)
    y2 = None
    for l in range(depth):
        last = l == depth - 1
        z, q, k, v, gates = _in_proj(h2, norm_mix_g[l][None], w_in[l].astype(BF16),
                                     b_gate[l][None], tm=256)
        os_, ls_ = zip(*[_attention_group(q, k, v, g, batch, seq, bq=128)
                         for g in range(len(ATTN_GROUPS))])
        h2, un = _mix(z, conv_dw_w[l], conv_dw_b[l][None], conv_ln_g[l][None], conv_ln_b[l][None],
                      w_conv_out[l].astype(BF16), os_, ls_, w_attn_out[l].astype(BF16), gates,
                      w_out[l].astype(BF16), h2, norm_ffn_g[l][None], seq, tm=256)
        assert last, "only the last layer's FFN output norm is implemented"
        y2 = _ffn(un, h2, w_up[l].astype(BF16), ffn_dw_w[l], ffn_dw_b[l][None],
                  w_down[l].astype(BF16), norm_final_g[None], seq, tm=256)
    return y2.reshape(batch, seq, d)
```

```python
import functools

import jax
import jax.numpy as jnp
import numpy as np
from jax import lax
from jax.experimental import pallas as pl
from jax.experimental.pallas import tpu as pltpu

F32 = jnp.float32
BF16 = jnp.bfloat16

D_MODEL = 1024
CONV_CH = 1024
CONV_WIDTH = 31
ATTN_GROUPS = ((128, 1), (512, 4), (2048, 16))
N_GROUPS = len(ATTN_GROUPS)
HEADS_PER_GROUP = 8
HEAD_DIM = 64
N_ATTN_HEADS = HEADS_PER_GROUP * N_GROUPS
ATTN_WIDTH = N_ATTN_HEADS * HEAD_DIM
SLOT_WIDTH = HEADS_PER_GROUP * HEAD_DIM
QKV_WIDTH = 3 * SLOT_WIDTH
D_FF = 2816
RMS_EPS = 1e-6
LN_EPS = 1e-5
MASK_VALUE = -1e30

LANES = 128
N_SLABS = D_MODEL // LANES
SLOT_SLABS = SLOT_WIDTH // LANES
HALF_BAND = 64
Q_SUB = 128
K_WIN = Q_SUB + 2 * HALF_BAND
HALO = 16
CONV_HALF = CONV_WIDTH // 2
VMEM_LIMIT = 56 * 1024 * 1024

_OFF_A, _OFF_G = 0, CONV_CH
_OFF_QKV = tuple(2 * CONV_CH + g * QKV_WIDTH for g in range(N_GROUPS))
_OFF_GATE = 2 * CONV_CH + N_GROUPS * QKV_WIDTH
_OFF_END = _OFF_GATE + 2 * D_MODEL

_SLOPES = (np.float32(2.0) ** (np.float32(-8.0) * np.arange(1, N_ATTN_HEADS + 1, dtype=np.float32)
                               / np.float32(N_ATTN_HEADS))).astype(np.float64)


def _sigmoid(x):
    return 1.0 / (1.0 + jnp.exp(-x))


def _silu(x):
    return x * _sigmoid(x)


def _rms_scale(xf):
    return lax.rsqrt(jnp.mean(xf * xf, axis=-1, keepdims=True) + RMS_EPS)


def _dot(a, b):
    return jnp.dot(a, b, preferred_element_type=F32)


def _resident(shape):
    return pl.BlockSpec(shape, lambda *_: (0,) * len(shape), pipeline_mode=pl.Buffered(1))


def _params(n_axes):
    return pltpu.CompilerParams(dimension_semantics=("arbitrary",) * n_axes,
                                vmem_limit_bytes=VMEM_LIMIT)


def _in_proj_kernel(x_ref, g_ref, w_ref, bg_ref, z_ref, gate_ref, qkv0_ref, qkv1_ref, qkv2_ref,
                    us_ref, *, tm):
    xf = x_ref[...]
    u = xf * _rms_scale(xf) * g_ref[...]
    ub = u.astype(BF16)

    def proj(lhs, lo, hi):
        return _dot(lhs, w_ref[:, lo:hi])

    z_ref[...] = (proj(ub, _OFF_A, _OFF_G) * _sigmoid(proj(ub, _OFF_G, _OFF_QKV[0]))).astype(BF16)
    gate_ref[...] = _sigmoid(proj(ub, _OFF_GATE, _OFF_END) + bg_ref[...]).astype(BF16)
    qkv0_ref[...] = proj(ub, _OFF_QKV[0], _OFF_QKV[0] + QKV_WIDTH).astype(BF16)

    for s in range(N_SLABS):
        us_ref[s] = u[:, s * LANES:(s + 1) * LANES]
    for out_ref, group in ((qkv1_ref, 1), (qkv2_ref, 2)):
        r = ATTN_GROUPS[group][1]
        n = tm // r
        lhs = jnp.concatenate(
            [jnp.concatenate([us_ref[s, pl.ds(c, n, stride=r), :] for s in range(N_SLABS)], axis=1)
             for c in range(r)], axis=0).astype(BF16)
        res = proj(lhs, _OFF_QKV[group], _OFF_QKV[group] + QKV_WIDTH).astype(BF16)
        for c in range(r):
            out_ref[c] = res[c * n:(c + 1) * n]


def _in_proj(x2, g, w_in, b_gate, batch, seq, tm):
    t = x2.shape[0]
    tiles = seq // tm
    row = lambda width: pl.BlockSpec((tm, width), lambda i: (i, 0))

    def class_major(r):
        return pl.BlockSpec((None, r, tm // r, QKV_WIDTH), lambda i: (i // tiles, 0, i % tiles, 0))

    r1, r2 = ATTN_GROUPS[1][1], ATTN_GROUPS[2][1]
    return pl.pallas_call(
        functools.partial(_in_proj_kernel, tm=tm),
        grid=(t // tm,),
        in_specs=[row(D_MODEL), _resident((1, D_MODEL)), _resident(w_in.shape),
                  _resident((1, 2 * D_MODEL))],
        out_specs=[row(CONV_CH), row(2 * D_MODEL), row(QKV_WIDTH), class_major(r1), class_major(r2)],
        out_shape=[jax.ShapeDtypeStruct((t, CONV_CH), BF16),
                   jax.ShapeDtypeStruct((t, 2 * D_MODEL), BF16),
                   jax.ShapeDtypeStruct((t, QKV_WIDTH), BF16),
                   jax.ShapeDtypeStruct((batch, r1, seq // r1, QKV_WIDTH), BF16),
                   jax.ShapeDtypeStruct((batch, r2, seq // r2, QKV_WIDTH), BF16)],
        scratch_shapes=[pltpu.VMEM((N_SLABS, tm, LANES), F32)],
        compiler_params=_params(1),
        name="in_proj",
    )(x2, g, w_in, b_gate)


def _attn_kernel(q_ref, k_ref, v_ref, o_ref, lse_ref, bias_ref, *, n_sub, length, slope_dist):
    n_blocks = length // Q_SUB

    @pl.when((pl.program_id(0) == 0) & (pl.program_id(1) == 0) & (pl.program_id(2) == 0))
    def _fill_bias():
        row = lax.broadcasted_iota(jnp.int32, (Q_SUB, K_WIN), 0)
        col = lax.broadcasted_iota(jnp.int32, (Q_SUB, K_WIN), 1)
        for e in range(3):
            dist = jnp.abs(col - row - (2 - e) * HALF_BAND)
            distf = dist.astype(F32)
            for h in range(HEADS_PER_GROUP):
                bias_ref[e, h] = jnp.where(dist <= HALF_BAND, -slope_dist[h] * distf, MASK_VALUE)

    low_half = lax.broadcasted_iota(jnp.int32, (1, LANES), 1) < HEAD_DIM

    def sub_block(j, carry):
        blk = pl.program_id(2) * n_sub + j
        m0 = blk * Q_SUB
        start = pl.multiple_of(jnp.clip(m0 - HALF_BAND, 0, length - K_WIN), HALF_BAND)
        edge = jnp.where(blk == 0, 2, jnp.where(blk == n_blocks - 1, 0, 1))
        r0 = pl.multiple_of(j * Q_SUB, Q_SUB)
        for pair in range(HEADS_PER_GROUP // 2):
            cols = slice(pair * LANES, (pair + 1) * LANES)
            qs = q_ref[pl.ds(r0, Q_SUB), cols]
            ks = k_ref[pl.ds(start, K_WIN), cols]
            vs = v_ref[pl.ds(start, K_WIN), cols]
            res = []
            for half in range(2):
                own = low_half if half == 0 else jnp.logical_not(low_half)
                qm = jnp.where(own, qs, jnp.zeros_like(qs))
                vm = jnp.where(own, vs, jnp.ones_like(vs))
                s = lax.dot_general(qm, ks, (((1,), (1,)), ((), ())), preferred_element_type=F32)
                s = s + bias_ref[edge, 2 * pair + half]
                m = jnp.max(s, axis=-1, keepdims=True)
                p = jnp.exp(s - m).astype(BF16)
                res.append((_dot(p, vm), m))
            (out_a, m_a), (out_b, m_b) = res
            num = jnp.where(low_half, out_a, out_b)
            den = pltpu.roll(jnp.where(low_half, out_b, out_a), HEAD_DIM, 1)
            o_ref[pl.ds(r0, Q_SUB), cols] = (num / den).astype(BF16)
            lse_ref[pl.ds(r0, Q_SUB), cols] = jnp.where(low_half, m_a, m_b) + jnp.log(den)
        return carry

    lax.fori_loop(0, n_sub, sub_block, 0)


def _attention_group(qkv, group, bq):
    batch, r, length, _ = qkv.shape
    bq = min(bq, length)
    slope_dist = tuple(float(_SLOPES[group * HEADS_PER_GROUP + h] * r) for h in range(HEADS_PER_GROUP))
    q_spec = pl.BlockSpec((None, None, bq, SLOT_WIDTH), lambda b, c, i: (b, c, i, 0))
    k_spec = pl.BlockSpec((None, None, length, SLOT_WIDTH), lambda b, c, i: (b, c, 0, 1))
    v_spec = pl.BlockSpec((None, None, length, SLOT_WIDTH), lambda b, c, i: (b, c, 0, 2))
    out_spec = pl.BlockSpec((None, None, bq, SLOT_WIDTH), lambda b, c, i: (b, c, i, 0))
    return pl.pallas_call(
        functools.partial(_attn_kernel, n_sub=bq // Q_SUB, length=length, slope_dist=slope_dist),
        grid=(batch, r, length // bq),
        in_specs=[q_spec, k_spec, v_spec],
        out_specs=[out_spec, out_spec],
        out_shape=[jax.ShapeDtypeStruct((batch, r, length, SLOT_WIDTH), BF16),
                   jax.ShapeDtypeStruct((batch, r, length, SLOT_WIDTH), F32)],
        scratch_shapes=[pltpu.VMEM((3, HEADS_PER_GROUP, Q_SUB, K_WIN), F32)],
        compiler_params=_params(3),
        name=f"attn_g{group}",
    )(qkv, qkv, qkv)


_CONV_ROWS = 128


def _interleave(src_ref, dst_ref, r, tm):
    n = tm // r
    for c in range(r):
        piece = src_ref[c].astype(F32)
        for s in range(SLOT_SLABS):
            dst_ref[s, pl.ds(c, n, stride=r), :] = piece[:, s * LANES:(s + 1) * LANES]
    return jnp.concatenate([dst_ref[s] for s in range(SLOT_SLABS)], axis=1)


def _mix_kernel(zp_ref, zc_ref, zn_ref, dww_ref, dwb_ref, lng_ref, lnb_ref, wco_ref,
                o0_ref, o1_ref, o2_ref, l0_ref, l1_ref, l2_ref, wao_ref,
                gate_ref, wout_ref, x_ref, gffn_ref,
                h_ref, un_ref, zs_ref, ys_ref, il_ref, *, tm, tiles_per_seq):
    pos = pl.program_id(0) % tiles_per_seq
    keep_prev = (pos != 0).astype(F32)
    keep_next = (pos != tiles_per_seq - 1).astype(F32)
    zp = zp_ref[...].astype(F32) * keep_prev
    zc = zc_ref[...].astype(F32)
    zn = zn_ref[...].astype(F32) * keep_next
    for s in range(N_SLABS):
        cols = slice(s * LANES, (s + 1) * LANES)
        zs_ref[s, 0:HALO, :] = zp[:, cols]
        zs_ref[s, HALO:HALO + tm, :] = zc[:, cols]
        zs_ref[s, HALO + tm:, :] = zn[:, cols]

    first = HALO - CONV_HALF
    half_rows = _CONV_ROWS // 2
    for s in range(N_SLABS):
        cols = slice(s * LANES, (s + 1) * LANES)
        taps = [dww_ref[j:j + 1, cols] for j in range(CONV_WIDTH)]
        bias = jnp.broadcast_to(dwb_ref[:, cols], (half_rows, LANES))
        for r0 in range(0, tm, _CONV_ROWS):
            acc_e, acc_o = bias, bias
            for k in range(CONV_WIDTH + 1):
                win = zs_ref[s, pl.ds(r0 + first + k, half_rows, stride=2), :]
                if k < CONV_WIDTH:
                    acc_e = acc_e + taps[k] * win
                if k > 0:
                    acc_o = acc_o + taps[k - 1] * win
            ys_ref[s, pl.ds(r0, half_rows, stride=2), :] = acc_e
            ys_ref[s, pl.ds(r0 + 1, half_rows, stride=2), :] = acc_o

    y = jnp.concatenate([ys_ref[s] for s in range(N_SLABS)], axis=1)
    mu = jnp.mean(y, axis=-1, keepdims=True)
    yc = y - mu
    var = jnp.mean(yc * yc, axis=-1, keepdims=True)
    yn = yc * lax.rsqrt(var + LN_EPS) * lng_ref[...] + lnb_ref[...]
    conv_out = _dot(_silu(yn).astype(BF16), wco_ref[...])

    r1, r2 = ATTN_GROUPS[1][1], ATTN_GROUPS[2][1]
    l0, o0 = l0_ref[...], o0_ref[...].astype(F32)
    l1 = _interleave(l1_ref, il_ref.at[0], r1, tm)
    o1 = _interleave(o1_ref, il_ref.at[1], r1, tm)
    l2 = _interleave(l2_ref, il_ref.at[2], r2, tm)
    o2 = _interleave(o2_ref, il_ref.at[3], r2, tm)
    mx = jnp.maximum(jnp.maximum(l0, l1), l2)
    e0, e1, e2 = jnp.exp(l0 - mx), jnp.exp(l1 - mx), jnp.exp(l2 - mx)
    attn = (e0 * o0 + e1 * o1 + e2 * o2) / (e0 + e1 + e2)
    attn_out = _dot(attn.astype(BF16), wao_ref[...])

    merged = (gate_ref[:, :D_MODEL].astype(F32) * conv_out
              + gate_ref[:, D_MODEL:].astype(F32) * attn_out)
    h = x_ref[...] + _dot(merged.astype(BF16), wout_ref[...])
    h_ref[...] = h
    un_ref[...] = (h * _rms_scale(h) * gffn_ref[...]).astype(BF16)


def _halo_specs(tm, width, n_rows):
    per = tm // HALO
    last = n_rows // HALO - 1
    prev = pl.BlockSpec((HALO, width), lambda i: (jnp.maximum(i * per - 1, 0), 0))
    cur = pl.BlockSpec((tm, width), lambda i: (i, 0))
    nxt = pl.BlockSpec((HALO, width), lambda i: (jnp.minimum((i + 1) * per, last), 0))
    return prev, cur, nxt


def _mix(z, dww, dwb, lng, lnb, wco, outs, lses, wao, gates, wout, x2, gffn, seq, tm):
    t = x2.shape[0]
    tiles = seq // tm
    row = lambda width: pl.BlockSpec((tm, width), lambda i: (i, 0))

    def class_major(r):
        return pl.BlockSpec((None, r, tm // r, SLOT_WIDTH), lambda i: (i // tiles, 0, i % tiles, 0))

    attn_specs = [row(SLOT_WIDTH)] + [class_major(ATTN_GROUPS[g][1]) for g in (1, 2)]
    flat0 = lambda a: a.reshape(t, SLOT_WIDTH)
    zp, zc, zn = _halo_specs(tm, CONV_CH, t)
    vec = _resident((1, D_MODEL))
    return pl.pallas_call(
        functools.partial(_mix_kernel, tm=tm, tiles_per_seq=tiles),
        grid=(t // tm,),
        in_specs=[zp, zc, zn, _resident(dww.shape), vec, vec, vec, _resident(wco.shape),
                  *attn_specs, *attn_specs, _resident(wao.shape),
                  row(2 * D_MODEL), _resident(wout.shape), row(D_MODEL), vec],
        out_specs=[row(D_MODEL), row(D_MODEL)],
        out_shape=[jax.ShapeDtypeStruct((t, D_MODEL), F32),
                   jax.ShapeDtypeStruct((t, D_MODEL), BF16)],
        scratch_shapes=[pltpu.VMEM((N_SLABS, tm + 2 * HALO, LANES), F32),
                        pltpu.VMEM((N_SLABS, tm, LANES), F32),
                        pltpu.VMEM((4, SLOT_SLABS, tm, LANES), F32)],
        compiler_params=_params(1),
        name="mix",
    )(z, z, z, dww, dwb, lng, lnb, wco, flat0(outs[0]), outs[1], outs[2],
      flat0(lses[0]), lses[1], lses[2], wao, gates, wout, x2, gffn)


_FF_CHUNK = 256


def _ffn_kernel(up_ref_prev, uc_ref, un_ref_next, h_ref, wup_ref, dww_ref, dwb_ref, wdn_ref,
                gfin_ref, y_ref, uext_ref, a_ref, b_ref, acc_ref, *, tm, tiles_per_seq):
    pos = pl.program_id(0) % tiles_per_seq
    uext_ref[0:HALO, :] = jnp.where(pos != 0, up_ref_prev[...], jnp.zeros_like(up_ref_prev))
    uext_ref[HALO:HALO + tm, :] = uc_ref[...]
    uext_ref[HALO + tm:, :] = jnp.where(pos != tiles_per_seq - 1, un_ref_next[...],
                                        jnp.zeros_like(un_ref_next))
    uext = uext_ref[...]

    def conv3(src_ref, col0):
        cols = slice(col0, col0 + _FF_CHUNK)
        return (dww_ref[0:1, cols] * src_ref[HALO - 1:HALO - 1 + tm, :]
                + dww_ref[1:2, cols] * src_ref[HALO:HALO + tm, :]
                + dww_ref[2:3, cols] * src_ref[HALO + 1:HALO + 1 + tm, :]
                + dwb_ref[:, cols])

    for c in range(D_FF // _FF_CHUNK):
        ca, cv = c * _FF_CHUNK, D_FF + c * _FF_CHUNK
        a_ref[...] = _dot(uext, wup_ref[:, ca:ca + _FF_CHUNK])
        b_ref[...] = _dot(uext, wup_ref[:, cv:cv + _FF_CHUNK])
        s = (_silu(conv3(a_ref, ca)) * conv3(b_ref, cv)).astype(BF16)
        part = _dot(s, wdn_ref[ca:ca + _FF_CHUNK, :])
        if c == 0:
            acc_ref[...] = part
        else:
            acc_ref[...] += part

    h = h_ref[...] + acc_ref[...]
    y_ref[...] = h * _rms_scale(h) * gfin_ref[...]


def _ffn(un, h, wup, dww, dwb, wdn, gfin, seq, tm):
    t = h.shape[0]
    row = lambda width: pl.BlockSpec((tm, width), lambda i: (i, 0))
    up, uc, unx = _halo_specs(tm, D_MODEL, t)
    return pl.pallas_call(
        functools.partial(_ffn_kernel, tm=tm, tiles_per_seq=seq // tm),
        grid=(t // tm,),
        in_specs=[up, uc, unx, row(D_MODEL), _resident(wup.shape), _resident(dww.shape),
                  _resident(dwb.shape), _resident(wdn.shape), _resident((1, D_MODEL))],
        out_specs=row(D_MODEL),
        out_shape=jax.ShapeDtypeStruct((t, D_MODEL), F32),
        scratch_shapes=[pltpu.VMEM((tm + 2 * HALO, D_MODEL), BF16),
                        pltpu.VMEM((tm + 2 * HALO, _FF_CHUNK), F32),
                        pltpu.VMEM((tm + 2 * HALO, _FF_CHUNK), F32),
                        pltpu.VMEM((tm, D_MODEL), F32)],
        compiler_params=_params(1),
        name="ffn",
    )(un, un, un, h, wup, dww, dwb, wdn, gfin)


def _reorder_w_in(w):
    a, glu, q, k, v, gates = jnp.split(
        w, np.cumsum((CONV_CH, CONV_CH, ATTN_WIDTH, ATTN_WIDTH, ATTN_WIDTH)).tolist(), axis=1)
    q = q * (HEAD_DIM ** -0.5)
    grp = lambda m, g: m[:, g * SLOT_WIDTH:(g + 1) * SLOT_WIDTH]
    qkv = [jnp.concatenate([grp(q, g), grp(k, g), grp(v, g)], axis=1) for g in range(N_GROUPS)]
    return jnp.concatenate([a, glu, *qkv, gates], axis=1).astype(BF16)


def kernel(x, norm_mix_g, w_in, b_gate, conv_dw_w, conv_dw_b, conv_ln_g, conv_ln_b, w_conv_out,
           w_attn_out, w_out, norm_ffn_g, w_up, ffn_dw_w, ffn_dw_b, w_down, norm_final_g):
    batch, seq, d = x.shape
    assert w_in.shape[0] == 1, "single-layer block"
    x2 = x.reshape(batch * seq, d)
    z, gates, qkv0, qkv1, qkv2 = _in_proj(x2, norm_mix_g, _reorder_w_in(w_in[0]), b_gate,
                                          batch, seq, tm=256)
    qkv0 = qkv0.reshape(batch, 1, seq, QKV_WIDTH)
    outs, lses = zip(*[_attention_group(qkv, g, bq=512) for g, qkv in enumerate((qkv0, qkv1, qkv2))])
    h2, un = _mix(z, conv_dw_w[0], conv_dw_b, conv_ln_g, conv_ln_b, w_conv_out[0].astype(BF16),
                  outs, lses, w_attn_out[0].astype(BF16), gates, w_out[0].astype(BF16), x2,
                  norm_ffn_g, seq, tm=256)
    y2 = _ffn(un, h2, w_up[0].astype(BF16), ffn_dw_w[0], ffn_dw_b, w_down[0].astype(BF16),
              norm_final_g[None], seq, tm=256)
    return y2.reshape(batch, seq, d)
```

```python
import functools

import jax
import jax.numpy as jnp
import numpy as np
from jax import lax
from jax.experimental import pallas as pl
from jax.experimental.pallas import tpu as pltpu

F32 = jnp.float32
BF16 = jnp.bfloat16

D_MODEL = 1024
CONV_CH = 1024
CONV_WIDTH = 31
ATTN_GROUPS = ((128, 1), (512, 4), (2048, 16))
N_GROUPS = len(ATTN_GROUPS)
HEADS_PER_GROUP = 8
HEAD_DIM = 64
N_ATTN_HEADS = HEADS_PER_GROUP * N_GROUPS
ATTN_WIDTH = N_ATTN_HEADS * HEAD_DIM
SLOT_WIDTH = HEADS_PER_GROUP * HEAD_DIM
QKV_WIDTH = 3 * SLOT_WIDTH
D_FF = 2816
RMS_EPS = 1e-6
LN_EPS = 1e-5
MASK_VALUE = -1e30
LOG2_E = float(np.log2(np.e))
LN_2 = float(np.log(2.0))

LANES = 128
N_SLABS = D_MODEL // LANES
SLOT_SLABS = SLOT_WIDTH // LANES
HALF_BAND = 64
Q_SUB = 128
K_WIN = Q_SUB + 2 * HALF_BAND
HALO = 16
CONV_HALF = CONV_WIDTH // 2
VMEM_LIMIT = 56 * 1024 * 1024

_OFF_A, _OFF_G = 0, CONV_CH
_OFF_QKV = tuple(2 * CONV_CH + g * QKV_WIDTH for g in range(N_GROUPS))
_OFF_GATE = 2 * CONV_CH + N_GROUPS * QKV_WIDTH
_OFF_END = _OFF_GATE + 2 * D_MODEL

_SLOPES = (np.float32(2.0) ** (np.float32(-8.0) * np.arange(1, N_ATTN_HEADS + 1, dtype=np.float32)
                               / np.float32(N_ATTN_HEADS))).astype(np.float64)


def _sigmoid(x):
    return 1.0 / (1.0 + jnp.exp(-x))


def _silu(x):
    return x * _sigmoid(x)


def _rms_scale(xf):
    return lax.rsqrt(jnp.mean(xf * xf, axis=-1, keepdims=True) + RMS_EPS)


def _dot(a, b):
    return jnp.dot(a, b, preferred_element_type=F32)


def _resident(shape):
    return pl.BlockSpec(shape, lambda *_: (0,) * len(shape), pipeline_mode=pl.Buffered(1))


def _params(n_axes):
    return pltpu.CompilerParams(dimension_semantics=("arbitrary",) * n_axes,
                                vmem_limit_bytes=VMEM_LIMIT)


_CONV_ROWS = 64


def _in_proj_kernel(x_ref, g_ref, w_ref, bg_ref, dww_ref, dwb_ref, lng_ref, lnb_ref,
                    cact_ref, gate_ref, qkv0_ref, qkv1_ref, qkv2_ref,
                    us_ref, zs_ref, ys_ref, *, tm, tiles_per_seq):
    step = pl.program_id(0)
    starts_seq = step % tiles_per_seq == 0

    @pl.when(step == 0)
    def _init():
        zs_ref[...] = jnp.zeros_like(zs_ref)

    xf = x_ref[...]
    u = xf * _rms_scale(xf) * g_ref[...]
    ub = u.astype(BF16)

    def proj(lhs, lo, hi):
        return _dot(lhs, w_ref[:, lo:hi])

    z = proj(ub, _OFF_A, _OFF_G) * _sigmoid(proj(ub, _OFF_G, _OFF_QKV[0]))
    z_head = jnp.where(starts_seq, 0.0, z[0:HALO])
    for s in range(N_SLABS):
        zs_ref[s, HALO + tm:, :] = z_head[:, s * LANES:(s + 1) * LANES]

    first = HALO - CONV_HALF
    half_rows = _CONV_ROWS // 2

    def conv_slab(s):
        cols = slice(s * LANES, (s + 1) * LANES)
        tap = lambda j: dww_ref[j:j + 1, cols]
        bias = jnp.broadcast_to(dwb_ref[:, cols], (half_rows, LANES))
        for r0 in range(0, tm, _CONV_ROWS):
            acc_e, acc_o = bias, bias
            for k in range(CONV_WIDTH + 1):
                win = zs_ref[s, pl.ds(r0 + first + k, half_rows, stride=2), :]
                if k < CONV_WIDTH:
                    acc_e = acc_e + tap(k) * win
                if k > 0:
                    acc_o = acc_o + tap(k - 1) * win
            ys_ref[s, pl.ds(r0, half_rows, stride=2), :] = acc_e
            ys_ref[s, pl.ds(r0 + 1, half_rows, stride=2), :] = acc_o

    q_cols = lax.broadcasted_iota(jnp.int32, (1, QKV_WIDTH), 1) < SLOT_WIDTH
    qkv_scale = jnp.where(q_cols, LOG2_E, 1.0).astype(F32)

    def qkv(lhs, group):
        return (proj(lhs, _OFF_QKV[group], _OFF_QKV[group] + QKV_WIDTH) * qkv_scale).astype(BF16)

    conv_slab(0)
    gate_ref[...] = _sigmoid(proj(ub, _OFF_GATE, _OFF_END) + bg_ref[...]).astype(BF16)
    conv_slab(1)
    conv_slab(2)
    qkv0_ref[...] = qkv(ub, 0)
    conv_slab(3)

    for s in range(N_SLABS):
        us_ref[s] = u[:, s * LANES:(s + 1) * LANES]
    for out_ref, group in ((qkv1_ref, 1), (qkv2_ref, 2)):
        r = ATTN_GROUPS[group][1]
        n = tm // r
        lhs = jnp.concatenate(
            [jnp.concatenate([us_ref[s, pl.ds(c, n, stride=r), :] for s in range(N_SLABS)], axis=1)
             for c in range(r)], axis=0).astype(BF16)
        conv_slab(2 + 2 * group)
        res = qkv(lhs, group)
        for c in range(r):
            out_ref[c] = res[c * n:(c + 1) * n]
        conv_slab(3 + 2 * group)

    y = jnp.concatenate([ys_ref[s] for s in range(N_SLABS)], axis=1)
    mu = jnp.mean(y, axis=-1, keepdims=True)
    yc = y - mu
    var = jnp.mean(yc * yc, axis=-1, keepdims=True)
    yn = yc * lax.rsqrt(var + LN_EPS) * lng_ref[...] + lnb_ref[...]
    cact_ref[...] = _silu(yn).astype(BF16)

    for s in range(N_SLABS):
        tail = zs_ref[s, tm:tm + HALO, :]
        zs_ref[s, 0:HALO, :] = jnp.where(starts_seq, 0.0, tail)
        zs_ref[s, HALO:HALO + tm, :] = z[:, s * LANES:(s + 1) * LANES]


def _in_proj(x2, g, w_in, b_gate, dww, dwb, lng, lnb, batch, seq, tm):
    t = x2.shape[0]
    tiles = seq // tm
    n_tiles = t // tm
    cur = lambda i: jnp.minimum(i, n_tiles - 1)
    row = lambda width: pl.BlockSpec((tm, width), lambda i: (cur(i), 0))
    lagged = pl.BlockSpec((tm, CONV_CH), lambda i: (jnp.maximum(i - 1, 0), 0))

    def class_major(r):
        return pl.BlockSpec((None, r, tm // r, QKV_WIDTH),
                            lambda i: (cur(i) // tiles, 0, cur(i) % tiles, 0))

    r1, r2 = ATTN_GROUPS[1][1], ATTN_GROUPS[2][1]
    vec = _resident((1, CONV_CH))
    return pl.pallas_call(
        functools.partial(_in_proj_kernel, tm=tm, tiles_per_seq=tiles),
        grid=(n_tiles + 1,),
        in_specs=[row(D_MODEL), _resident((1, D_MODEL)), _resident(w_in.shape),
                  _resident((1, 2 * D_MODEL)), _resident(dww.shape), vec, vec, vec],
        out_specs=[lagged, row(2 * D_MODEL), row(QKV_WIDTH), class_major(r1), class_major(r2)],
        out_shape=[jax.ShapeDtypeStruct((t, CONV_CH), BF16),
                   jax.ShapeDtypeStruct((t, 2 * D_MODEL), BF16),
                   jax.ShapeDtypeStruct((t, QKV_WIDTH), BF16),
                   jax.ShapeDtypeStruct((batch, r1, seq // r1, QKV_WIDTH), BF16),
                   jax.ShapeDtypeStruct((batch, r2, seq // r2, QKV_WIDTH), BF16)],
        scratch_shapes=[pltpu.VMEM((N_SLABS, tm, LANES), F32),
                        pltpu.VMEM((N_SLABS, tm + 2 * HALO, LANES), F32),
                        pltpu.VMEM((N_SLABS, tm, LANES), F32)],
        compiler_params=_params(1),
        name="in_proj",
    )(x2, g, w_in, b_gate, dww, dwb, lng, lnb)


def _attn_kernel(q_ref, k_ref, v_ref, o_ref, lse_ref, bias_ref, *, n_sub, length, slope_dist):
    n_blocks = length // Q_SUB

    @pl.when((pl.program_id(0) == 0) & (pl.program_id(1) == 0) & (pl.program_id(2) == 0))
    def _fill_bias():
        row = lax.broadcasted_iota(jnp.int32, (Q_SUB, K_WIN), 0)
        col = lax.broadcasted_iota(jnp.int32, (Q_SUB, K_WIN), 1)
        for e in range(3):
            dist = jnp.abs(col - row - (2 - e) * HALF_BAND)
            distf = dist.astype(F32)
            for h in range(HEADS_PER_GROUP):
                bias_ref[e, h // 2, (h % 2) * Q_SUB:(h % 2 + 1) * Q_SUB, :] = jnp.where(
                    dist <= HALF_BAND, (-LOG2_E * slope_dist[h]) * distf, MASK_VALUE)

    low_half = lax.broadcasted_iota(jnp.int32, (1, LANES), 1) < HEAD_DIM

    def sub_block(j, carry):
        blk = pl.program_id(2) * n_sub + j
        m0 = blk * Q_SUB
        start = pl.multiple_of(jnp.clip(m0 - HALF_BAND, 0, length - K_WIN), HALF_BAND)
        edge = jnp.where(blk == 0, 2, jnp.where(blk == n_blocks - 1, 0, 1))
        r0 = pl.multiple_of(j * Q_SUB, Q_SUB)
        for pair in range(HEADS_PER_GROUP // 2):
            cols = slice(pair * LANES, (pair + 1) * LANES)
            qs = q_ref[pl.ds(r0, Q_SUB), cols]
            ks = k_ref[pl.ds(start, K_WIN), cols]
            vs = v_ref[pl.ds(start, K_WIN), cols]
            zero = jnp.zeros_like(qs)
            q2 = jnp.concatenate([jnp.where(low_half, qs, zero), jnp.where(low_half, zero, qs)], axis=0)
            s = lax.dot_general(q2, ks, (((1,), (1,)), ((), ())), preferred_element_type=F32)
            s = s + bias_ref[edge, pair]
            m = jnp.max(s, axis=-1, keepdims=True)
            p = jnp.exp2(s - m).astype(BF16)
            out = _dot(p, jnp.concatenate([vs, jnp.ones_like(vs)], axis=1))
            num = jnp.where(low_half, out[:Q_SUB, :LANES], out[Q_SUB:, :LANES])
            den = jnp.where(low_half, out[:Q_SUB, LANES:], out[Q_SUB:, LANES:])
            m2 = jnp.where(low_half, m[:Q_SUB], m[Q_SUB:])
            o_ref[pl.ds(r0, Q_SUB), cols] = (num / den).astype(BF16)
            lse_ref[pl.ds(r0, Q_SUB), cols] = (m2 + jnp.log2(den)) * LN_2
        return carry

    lax.fori_loop(0, n_sub, sub_block, 0, unroll=True)


def _attention_group(qkv, group, bq):
    batch, r, length, _ = qkv.shape
    bq = min(bq, length)
    slope_dist = tuple(float(_SLOPES[group * HEADS_PER_GROUP + h] * r) for h in range(HEADS_PER_GROUP))
    q_spec = pl.BlockSpec((None, None, bq, SLOT_WIDTH), lambda b, c, i: (b, c, i, 0))
    k_spec = pl.BlockSpec((None, None, length, SLOT_WIDTH), lambda b, c, i: (b, c, 0, 1))
    v_spec = pl.BlockSpec((None, None, length, SLOT_WIDTH), lambda b, c, i: (b, c, 0, 2))
    out_spec = pl.BlockSpec((None, None, bq, SLOT_WIDTH), lambda b, c, i: (b, c, i, 0))
    return pl.pallas_call(
        functools.partial(_attn_kernel, n_sub=bq // Q_SUB, length=length, slope_dist=slope_dist),
        grid=(batch, r, length // bq),
        in_specs=[q_spec, k_spec, v_spec],
        out_specs=[out_spec, out_spec],
        out_shape=[jax.ShapeDtypeStruct((batch, r, length, SLOT_WIDTH), BF16),
                   jax.ShapeDtypeStruct((batch, r, length, SLOT_WIDTH), F32)],
        scratch_shapes=[pltpu.VMEM((3, HEADS_PER_GROUP // 2, 2 * Q_SUB, K_WIN), F32)],
        compiler_params=_params(3),
        name=f"attn_g{group}",
    )(qkv, qkv, qkv)


def _interleave(src_ref, dst_ref, r, tm):
    n = tm // r
    for c in range(r):
        piece = src_ref[c].astype(F32)
        for s in range(SLOT_SLABS):
            dst_ref[s, pl.ds(c, n, stride=r), :] = piece[:, s * LANES:(s + 1) * LANES]
    return jnp.concatenate([dst_ref[s] for s in range(SLOT_SLABS)], axis=1)


def _mix_kernel(cact_ref, wco_ref, o0_ref, o1_ref, o2_ref, l0_ref, l1_ref, l2_ref, wao_ref,
                gate_ref, wout_ref, x_ref, gffn_ref, h_ref, un_ref, il_ref, *, tm):
    conv_out = _dot(cact_ref[...], wco_ref[...])

    r1, r2 = ATTN_GROUPS[1][1], ATTN_GROUPS[2][1]
    l0, o0 = l0_ref[...], o0_ref[...].astype(F32)
    l1 = _interleave(l1_ref, il_ref.at[0], r1, tm)
    o1 = _interleave(o1_ref, il_ref.at[1], r1, tm)
    l2 = _interleave(l2_ref, il_ref.at[2], r2, tm)
    o2 = _interleave(o2_ref, il_ref.at[3], r2, tm)
    mx = jnp.maximum(jnp.maximum(l0, l1), l2)
    e0, e1, e2 = jnp.exp(l0 - mx), jnp.exp(l1 - mx), jnp.exp(l2 - mx)
    attn = (e0 * o0 + e1 * o1 + e2 * o2) / (e0 + e1 + e2)
    attn_out = _dot(attn.astype(BF16), wao_ref[...])

    merged = (gate_ref[:, :D_MODEL].astype(F32) * conv_out
              + gate_ref[:, D_MODEL:].astype(F32) * attn_out)
    h = x_ref[...] + _dot(merged.astype(BF16), wout_ref[...])
    h_ref[...] = h
    un_ref[...] = (h * _rms_scale(h) * gffn_ref[...]).astype(BF16)


def _mix(cact, wco, outs, lses, wao, gates, wout, x2, gffn, seq, tm):
    t = x2.shape[0]
    tiles = seq // tm
    row = lambda width: pl.BlockSpec((tm, width), lambda i: (i, 0))

    def class_major(r):
        return pl.BlockSpec((None, r, tm // r, SLOT_WIDTH), lambda i: (i // tiles, 0, i % tiles, 0))

    attn_specs = [row(SLOT_WIDTH)] + [class_major(ATTN_GROUPS[g][1]) for g in (1, 2)]
    flat0 = lambda a: a.reshape(t, SLOT_WIDTH)
    return pl.pallas_call(
        functools.partial(_mix_kernel, tm=tm),
        grid=(t // tm,),
        in_specs=[row(CONV_CH), _resident(wco.shape), *attn_specs, *attn_specs,
                  _resident(wao.shape), row(2 * D_MODEL), _resident(wout.shape), row(D_MODEL),
                  _resident((1, D_MODEL))],
        out_specs=[row(D_MODEL), row(D_MODEL)],
        out_shape=[jax.ShapeDtypeStruct((t, D_MODEL), F32),
                   jax.ShapeDtypeStruct((t, D_MODEL), BF16)],
        scratch_shapes=[pltpu.VMEM((4, SLOT_SLABS, tm, LANES), F32)],
        compiler_params=_params(1),
        name="mix",
    )(cact, wco, flat0(outs[0]), outs[1], outs[2], flat0(lses[0]), lses[1], lses[2], wao, gates,
      wout, x2, gffn)


_FF_CHUNK = 256
_FF_SLABS = _FF_CHUNK // LANES


def _halo_specs(tm, width, n_rows):
    per = tm // HALO
    last = n_rows // HALO - 1
    prev = pl.BlockSpec((HALO, width), lambda i: (jnp.maximum(i * per - 1, 0), 0))
    cur = pl.BlockSpec((tm, width), lambda i: (i, 0))
    nxt = pl.BlockSpec((HALO, width), lambda i: (jnp.minimum((i + 1) * per, last), 0))
    return prev, cur, nxt


def _ffn_kernel(up_ref_prev, uc_ref, un_ref_next, h_ref, wup_ref, dww_ref, dwb_ref, wdn_ref,
                gfin_ref, y_ref, uext_ref, ab_ref, acc_ref, perm_ref, *, tm, tiles_per_seq):
    pos = pl.program_id(0) % tiles_per_seq
    uext_ref[0:HALO, :] = jnp.where(pos != 0, up_ref_prev[...], jnp.zeros_like(up_ref_prev))
    uext_ref[HALO:HALO + tm, :] = uc_ref[...]
    uext_ref[HALO + tm:, :] = jnp.where(pos != tiles_per_seq - 1, un_ref_next[...],
                                        jnp.zeros_like(un_ref_next))
    uext = uext_ref[...]
    n_chunks = D_FF // _FF_CHUNK
    half_rows = tm // 2

    def up_proj(c):
        for half in range(2):
            col0 = half * D_FF + c * _FF_CHUNK
            res = _dot(uext, wup_ref[:, col0:col0 + _FF_CHUNK])
            for sl in range(_FF_SLABS):
                ab_ref[c % 2, half, sl] = res[:, sl * LANES:(sl + 1) * LANES]

    def conv3_even_odd(src_ref, col0):
        cols = slice(col0, col0 + LANES)
        w0, w1, w2 = (dww_ref[j:j + 1, cols] for j in range(3))
        bias = dwb_ref[:, cols]
        win = [src_ref[pl.ds(HALO - 1 + k, half_rows, stride=2), :] for k in range(4)]
        even = w0 * win[0] + w1 * win[1] + w2 * win[2] + bias
        odd = w0 * win[1] + w1 * win[2] + w2 * win[3] + bias
        return even, odd

    up_proj(0)
    for c in range(n_chunks):
        if c + 1 < n_chunks:
            up_proj(c + 1)
        ca = c * _FF_CHUNK
        evens, odds = [], []
        for sl in range(_FF_SLABS):
            a_e, a_o = conv3_even_odd(ab_ref.at[c % 2, 0, sl], ca + sl * LANES)
            v_e, v_o = conv3_even_odd(ab_ref.at[c % 2, 1, sl], D_FF + ca + sl * LANES)
            evens.append((_silu(a_e) * v_e).astype(BF16))
            odds.append((_silu(a_o) * v_o).astype(BF16))
        s = jnp.concatenate([jnp.concatenate(evens, axis=1), jnp.concatenate(odds, axis=1)], axis=0)
        part = _dot(s, wdn_ref[ca:ca + _FF_CHUNK, :])
        if c == 0:
            acc_ref[...] = part
        else:
            acc_ref[...] += part

    for sl in range(N_SLABS):
        cols = slice(sl * LANES, (sl + 1) * LANES)
        perm_ref[sl, pl.ds(0, half_rows, stride=2), :] = acc_ref[0:half_rows, cols]
        perm_ref[sl, pl.ds(1, half_rows, stride=2), :] = acc_ref[half_rows:tm, cols]
    h = h_ref[...] + jnp.concatenate([perm_ref[sl] for sl in range(N_SLABS)], axis=1)
    y_ref[...] = h * _rms_scale(h) * gfin_ref[...]


def _ffn(un, h, wup, dww, dwb, wdn, gfin, seq, tm):
    t = h.shape[0]
    row = lambda width: pl.BlockSpec((tm, width), lambda i: (i, 0))
    up, uc, unx = _halo_specs(tm, D_MODEL, t)
    return pl.pallas_call(
        functools.partial(_ffn_kernel, tm=tm, tiles_per_seq=seq // tm),
        grid=(t // tm,),
        in_specs=[up, uc, unx, row(D_MODEL), _resident(wup.shape), _resident(dww.shape),
                  _resident(dwb.shape), _resident(wdn.shape), _resident((1, D_MODEL))],
        out_specs=row(D_MODEL),
        out_shape=jax.ShapeDtypeStruct((t, D_MODEL), F32),
        scratch_shapes=[pltpu.VMEM((tm + 2 * HALO, D_MODEL), BF16),
                        pltpu.VMEM((2, 2, _FF_SLABS, tm + 2 * HALO, LANES), F32),
                        pltpu.VMEM((tm, D_MODEL), F32),
                        pltpu.VMEM((N_SLABS, tm, LANES), F32)],
        compiler_params=_params(1),
        name="ffn",
    )(un, un, un, h, wup, dww, dwb, wdn, gfin)


def _reorder_w_in(w):
    a, glu, q, k, v, gates = jnp.split(
        w, np.cumsum((CONV_CH, CONV_CH, ATTN_WIDTH, ATTN_WIDTH, ATTN_WIDTH)).tolist(), axis=1)
    q = q * (HEAD_DIM ** -0.5)
    grp = lambda m, g: m[:, g * SLOT_WIDTH:(g + 1) * SLOT_WIDTH]
    qkv = [jnp.concatenate([grp(q, g), grp(k, g), grp(v, g)], axis=1) for g in range(N_GROUPS)]
    return jnp.concatenate([a, glu, *qkv, gates], axis=1).astype(BF16)


def kernel(x, norm_mix_g, w_in, b_gate, conv_dw_w, conv_dw_b, conv_ln_g, conv_ln_b, w_conv_out,
           w_attn_out, w_out, norm_ffn_g, w_up, ffn_dw_w, ffn_dw_b, w_down, norm_final_g):
    batch, seq, d = x.shape
    assert w_in.shape[0] == 1, "single-layer block"
    x2 = x.reshape(batch * seq, d)
    cact, gates, qkv0, qkv1, qkv2 = _in_proj(
        x2, norm_mix_g, _reorder_w_in(w_in[0]), b_gate, conv_dw_w[0], conv_dw_b, conv_ln_g,
        conv_ln_b, batch, seq, tm=256)
    qkv0 = qkv0.reshape(batch, 1, seq, QKV_WIDTH)
    outs, lses = zip(*[_attention_group(qkv, g, bq=512) for g, qkv in enumerate((qkv0, qkv1, qkv2))])
    h2, un = _mix(cact, w_conv_out[0].astype(BF16), outs, lses, w_attn_out[0].astype(BF16), gates,
                  w_out[0].astype(BF16), x2, norm_ffn_g, seq, tm=256)
    y2 = _ffn(un, h2, w_up[0].astype(BF16), ffn_dw_w[0], ffn_dw_b, w_down[0].astype(BF16),
              norm_final_g[None], seq, tm=512)
    return y2.reshape(batch, seq, d)
```

```python
import functools

import jax
import jax.numpy as jnp
import numpy as np
from jax import lax
from jax.experimental import pallas as pl
from jax.experimental.pallas import tpu as pltpu

F32 = jnp.float32
BF16 = jnp.bfloat16

D_MODEL = 1024
CONV_CH = 1024
CONV_WIDTH = 31
ATTN_GROUPS = ((128, 1), (512, 4), (2048, 16))
N_GROUPS = len(ATTN_GROUPS)
HEADS_PER_GROUP = 8
HEAD_DIM = 64
N_ATTN_HEADS = HEADS_PER_GROUP * N_GROUPS
ATTN_WIDTH = N_ATTN_HEADS * HEAD_DIM
SLOT_WIDTH = HEADS_PER_GROUP * HEAD_DIM
QKV_WIDTH = 3 * SLOT_WIDTH
D_FF = 2816
RMS_EPS = 1e-6
LN_EPS = 1e-5
MASK_VALUE = -1e30
LOG2_E = float(np.log2(np.e))
LN_2 = float(np.log(2.0))

LANES = 128
N_SLABS = D_MODEL // LANES
SLOT_SLABS = SLOT_WIDTH // LANES
HALF_BAND = 64
Q_SUB = 128
K_WIN = Q_SUB + 2 * HALF_BAND
HALO = 16
CONV_HALF = CONV_WIDTH // 2
VMEM_LIMIT = 56 * 1024 * 1024

_OFF_A, _OFF_G = 0, CONV_CH
_OFF_QKV = tuple(2 * CONV_CH + g * QKV_WIDTH for g in range(N_GROUPS))
_OFF_GATE = 2 * CONV_CH + N_GROUPS * QKV_WIDTH
_OFF_END = _OFF_GATE + 2 * D_MODEL

_SLOPES = (np.float32(2.0) ** (np.float32(-8.0) * np.arange(1, N_ATTN_HEADS + 1, dtype=np.float32)
                               / np.float32(N_ATTN_HEADS))).astype(np.float64)


def _sigmoid(x):
    return 0.5 * jnp.tanh(0.5 * x) + 0.5


def _silu(x):
    return x * _sigmoid(x)


def _rms_scale(xf):
    return lax.rsqrt(jnp.mean(xf * xf, axis=-1, keepdims=True) + RMS_EPS)


def _dot(a, b):
    return jnp.dot(a, b, preferred_element_type=F32)


def _resident(shape):
    return pl.BlockSpec(shape, lambda *_: (0,) * len(shape), pipeline_mode=pl.Buffered(1))


def _params(n_axes):
    return pltpu.CompilerParams(dimension_semantics=("arbitrary",) * n_axes,
                                vmem_limit_bytes=VMEM_LIMIT)


_CONV_ROWS = 64


def _in_proj_kernel(x_ref, g_ref, w_ref, bg_ref, dww_ref, dwb_ref, lng_ref, lnb_ref,
                    cact_ref, gate_ref, qkv0_ref, qkv1_ref, qkv2_ref,
                    us_ref, zs_ref, ys_ref, *, tm, tiles_per_seq):
    step = pl.program_id(0)
    starts_seq = step % tiles_per_seq == 0

    @pl.when(step == 0)
    def _init():
        zs_ref[...] = jnp.zeros_like(zs_ref)

    xf = x_ref[...]
    u = xf * _rms_scale(xf) * g_ref[...]
    ub = u.astype(BF16)

    def proj(lhs, lo, hi):
        return _dot(lhs, w_ref[:, lo:hi])

    z = proj(ub, _OFF_A, _OFF_G) * _sigmoid(proj(ub, _OFF_G, _OFF_QKV[0]))
    z_head = jnp.where(starts_seq, 0.0, z[0:HALO])
    for s in range(N_SLABS):
        zs_ref[s, HALO + tm:, :] = z_head[:, s * LANES:(s + 1) * LANES]

    first = HALO - CONV_HALF
    half_rows = _CONV_ROWS // 2

    def conv_slab(s):
        cols = slice(s * LANES, (s + 1) * LANES)
        tap = lambda j: dww_ref[j:j + 1, cols]
        bias = jnp.broadcast_to(dwb_ref[:, cols], (half_rows, LANES))
        for r0 in range(0, tm, _CONV_ROWS):
            acc_e, acc_o = bias, bias
            for k in range(CONV_WIDTH + 1):
                win = zs_ref[s, pl.ds(r0 + first + k, half_rows, stride=2), :]
                if k < CONV_WIDTH:
                    acc_e = acc_e + tap(k) * win
                if k > 0:
                    acc_o = acc_o + tap(k - 1) * win
            ys_ref[s, pl.ds(r0, half_rows, stride=2), :] = acc_e
            ys_ref[s, pl.ds(r0 + 1, half_rows, stride=2), :] = acc_o

    q_cols = lax.broadcasted_iota(jnp.int32, (1, QKV_WIDTH), 1) < SLOT_WIDTH
    qkv_scale = jnp.where(q_cols, LOG2_E, 1.0).astype(F32)

    def qkv(lhs, group):
        return (proj(lhs, _OFF_QKV[group], _OFF_QKV[group] + QKV_WIDTH) * qkv_scale).astype(BF16)

    for s in range(N_SLABS):
        conv_slab(s)
    gate_ref[...] = _sigmoid(proj(ub, _OFF_GATE, _OFF_END) + bg_ref[...]).astype(BF16)
    qkv0_ref[...] = qkv(ub, 0)

    for s in range(N_SLABS):
        us_ref[s] = u[:, s * LANES:(s + 1) * LANES]
    for out_ref, group in ((qkv1_ref, 1), (qkv2_ref, 2)):
        r = ATTN_GROUPS[group][1]
        n = tm // r
        lhs = jnp.concatenate(
            [jnp.concatenate([us_ref[s, pl.ds(c, n, stride=r), :] for s in range(N_SLABS)], axis=1)
             for c in range(r)], axis=0).astype(BF16)
        res = qkv(lhs, group)
        for c in range(r):
            out_ref[c] = res[c * n:(c + 1) * n]

    y = jnp.concatenate([ys_ref[s] for s in range(N_SLABS)], axis=1)
    mu = jnp.mean(y, axis=-1, keepdims=True)
    yc = y - mu
    var = jnp.mean(yc * yc, axis=-1, keepdims=True)
    yn = yc * lax.rsqrt(var + LN_EPS) * lng_ref[...] + lnb_ref[...]
    cact_ref[...] = _silu(yn).astype(BF16)

    for s in range(N_SLABS):
        tail = zs_ref[s, tm:tm + HALO, :]
        zs_ref[s, 0:HALO, :] = jnp.where(starts_seq, 0.0, tail)
        zs_ref[s, HALO:HALO + tm, :] = z[:, s * LANES:(s + 1) * LANES]


def _in_proj(x2, g, w_in, b_gate, dww, dwb, lng, lnb, batch, seq, tm):
    t = x2.shape[0]
    tiles = seq // tm
    n_tiles = t // tm
    cur = lambda i: jnp.minimum(i, n_tiles - 1)
    row = lambda width: pl.BlockSpec((tm, width), lambda i: (cur(i), 0))
    lagged = pl.BlockSpec((tm, CONV_CH), lambda i: (jnp.maximum(i - 1, 0), 0))

    def class_major(r):
        return pl.BlockSpec((None, r, tm // r, QKV_WIDTH),
                            lambda i: (cur(i) // tiles, 0, cur(i) % tiles, 0))

    r1, r2 = ATTN_GROUPS[1][1], ATTN_GROUPS[2][1]
    vec = _resident((1, CONV_CH))
    return pl.pallas_call(
        functools.partial(_in_proj_kernel, tm=tm, tiles_per_seq=tiles),
        grid=(n_tiles + 1,),
        in_specs=[row(D_MODEL), _resident((1, D_MODEL)), _resident(w_in.shape),
                  _resident((1, 2 * D_MODEL)), _resident(dww.shape), vec, vec, vec],
        out_specs=[lagged, row(2 * D_MODEL), row(QKV_WIDTH), class_major(r1), class_major(r2)],
        out_shape=[jax.ShapeDtypeStruct((t, CONV_CH), BF16),
                   jax.ShapeDtypeStruct((t, 2 * D_MODEL), BF16),
                   jax.ShapeDtypeStruct((t, QKV_WIDTH), BF16),
                   jax.ShapeDtypeStruct((batch, r1, seq // r1, QKV_WIDTH), BF16),
                   jax.ShapeDtypeStruct((batch, r2, seq // r2, QKV_WIDTH), BF16)],
        scratch_shapes=[pltpu.VMEM((N_SLABS, tm, LANES), F32),
                        pltpu.VMEM((N_SLABS, tm + 2 * HALO, LANES), F32),
                        pltpu.VMEM((N_SLABS, tm, LANES), F32)],
        compiler_params=_params(1),
        name="in_proj",
    )(x2, g, w_in, b_gate, dww, dwb, lng, lnb)


def _attn_kernel(q_ref, k_ref, v_ref, o_ref, lse_ref, bias_ref, *, n_cls, n_sub, length,
                 slope_dist):
    n_blocks = length // Q_SUB

    @pl.when((pl.program_id(0) == 0) & (pl.program_id(1) == 0) & (pl.program_id(2) == 0))
    def _fill_bias():
        row = lax.broadcasted_iota(jnp.int32, (Q_SUB, K_WIN), 0)
        col = lax.broadcasted_iota(jnp.int32, (Q_SUB, K_WIN), 1)
        for e in range(3):
            dist = jnp.abs(col - row - (2 - e) * HALF_BAND)
            distf = dist.astype(F32)
            for h in range(HEADS_PER_GROUP):
                bias_ref[e, h // 2, (h % 2) * Q_SUB:(h % 2 + 1) * Q_SUB, :] = jnp.where(
                    dist <= HALF_BAND, (-LOG2_E * slope_dist[h]) * distf, MASK_VALUE)

    low_half = lax.broadcasted_iota(jnp.int32, (1, LANES), 1) < HEAD_DIM

    def sub_block(cls, j):
        blk = pl.program_id(2) * n_sub + j
        m0 = blk * Q_SUB
        start = pl.multiple_of(jnp.clip(m0 - HALF_BAND, 0, length - K_WIN), HALF_BAND)
        edge = jnp.where(blk == 0, 2, jnp.where(blk == n_blocks - 1, 0, 1))
        rows = slice(j * Q_SUB, (j + 1) * Q_SUB)
        for pair in range(HEADS_PER_GROUP // 2):
            cols = slice(pair * LANES, (pair + 1) * LANES)
            qs = q_ref[cls, rows, cols]
            ks = k_ref[cls, pl.ds(start, K_WIN), cols]
            vs = v_ref[cls, pl.ds(start, K_WIN), cols]
            zero = jnp.zeros_like(qs)
            q2 = jnp.concatenate([jnp.where(low_half, qs, zero), jnp.where(low_half, zero, qs)], axis=0)
            s = lax.dot_general(q2, ks, (((1,), (1,)), ((), ())), preferred_element_type=F32)
            s = s + bias_ref[edge, pair]
            m = jnp.max(s, axis=-1, keepdims=True)
            p = jnp.exp2(s - m).astype(BF16)
            out = _dot(p, jnp.concatenate([vs, jnp.ones_like(vs)], axis=1))
            num = jnp.where(low_half, out[:Q_SUB, :LANES], out[Q_SUB:, :LANES])
            den = jnp.where(low_half, out[:Q_SUB, LANES:], out[Q_SUB:, LANES:])
            m2 = jnp.where(low_half, m[:Q_SUB], m[Q_SUB:])
            o_ref[cls, rows, cols] = (num / den).astype(BF16)
            lse_ref[cls, rows, cols] = (m2 + jnp.log2(den)) * LN_2

    for cls in range(n_cls):
        for j in range(n_sub):
            sub_block(cls, j)


def _attention_group(qkv, group, queries_per_step):
    batch, r, length, _ = qkv.shape
    bq = min(queries_per_step, length)
    n_cls = queries_per_step // bq
    slope_dist = tuple(float(_SLOPES[group * HEADS_PER_GROUP + h] * r) for h in range(HEADS_PER_GROUP))
    q_spec = pl.BlockSpec((None, n_cls, bq, SLOT_WIDTH), lambda b, c, i: (b, c, i, 0))
    k_spec = pl.BlockSpec((None, n_cls, length, SLOT_WIDTH), lambda b, c, i: (b, c, 0, 1))
    v_spec = pl.BlockSpec((None, n_cls, length, SLOT_WIDTH), lambda b, c, i: (b, c, 0, 2))
    out_spec = pl.BlockSpec((None, n_cls, bq, SLOT_WIDTH), lambda b, c, i: (b, c, i, 0))
    return pl.pallas_call(
        functools.partial(_attn_kernel, n_cls=n_cls, n_sub=bq // Q_SUB, length=length,
                          slope_dist=slope_dist),
        grid=(batch, r // n_cls, length // bq),
        in_specs=[q_spec, k_spec, v_spec],
        out_specs=[out_spec, out_spec],
        out_shape=[jax.ShapeDtypeStruct((batch, r, length, SLOT_WIDTH), BF16),
                   jax.ShapeDtypeStruct((batch, r, length, SLOT_WIDTH), F32)],
        scratch_shapes=[pltpu.VMEM((3, HEADS_PER_GROUP // 2, 2 * Q_SUB, K_WIN), F32)],
        compiler_params=_params(3),
        name=f"attn_g{group}",
    )(qkv, qkv, qkv)


def _interleave(src_ref, dst_ref, r, tm):
    n = tm // r
    for c in range(r):
        piece = src_ref[c].astype(F32)
        for s in range(SLOT_SLABS):
            dst_ref[s, pl.ds(c, n, stride=r), :] = piece[:, s * LANES:(s + 1) * LANES]
    return jnp.concatenate([dst_ref[s] for s in range(SLOT_SLABS)], axis=1)


def _mix_kernel(cact_ref, wco_ref, o0_ref, o1_ref, o2_ref, l0_ref, l1_ref, l2_ref, wao_ref,
                gate_ref, wout_ref, x_ref, gffn_ref, h_ref, un_ref, il_ref, *, tm):
    conv_out = _dot(cact_ref[...], wco_ref[...])

    r1, r2 = ATTN_GROUPS[1][1], ATTN_GROUPS[2][1]
    l0, o0 = l0_ref[...], o0_ref[...].astype(F32)
    l1 = _interleave(l1_ref, il_ref.at[0], r1, tm)
    o1 = _interleave(o1_ref, il_ref.at[1], r1, tm)
    l2 = _interleave(l2_ref, il_ref.at[2], r2, tm)
    o2 = _interleave(o2_ref, il_ref.at[3], r2, tm)
    mx = jnp.maximum(jnp.maximum(l0, l1), l2)
    e0, e1, e2 = jnp.exp(l0 - mx), jnp.exp(l1 - mx), jnp.exp(l2 - mx)
    attn = (e0 * o0 + e1 * o1 + e2 * o2) / (e0 + e1 + e2)
    attn_out = _dot(attn.astype(BF16), wao_ref[...])

    merged = (gate_ref[:, :D_MODEL].astype(F32) * conv_out
              + gate_ref[:, D_MODEL:].astype(F32) * attn_out)
    h = x_ref[...] + _dot(merged.astype(BF16), wout_ref[...])
    h_ref[...] = h
    un_ref[...] = (h * _rms_scale(h) * gffn_ref[...]).astype(BF16)


def _mix(cact, wco, outs, lses, wao, gates, wout, x2, gffn, seq, tm):
    t = x2.shape[0]
    tiles = seq // tm
    row = lambda width: pl.BlockSpec((tm, width), lambda i: (i, 0))

    def class_major(r):
        return pl.BlockSpec((None, r, tm // r, SLOT_WIDTH), lambda i: (i // tiles, 0, i % tiles, 0))

    attn_specs = [row(SLOT_WIDTH)] + [class_major(ATTN_GROUPS[g][1]) for g in (1, 2)]
    flat0 = lambda a: a.reshape(t, SLOT_WIDTH)
    return pl.pallas_call(
        functools.partial(_mix_kernel, tm=tm),
        grid=(t // tm,),
        in_specs=[row(CONV_CH), _resident(wco.shape), *attn_specs, *attn_specs,
                  _resident(wao.shape), row(2 * D_MODEL), _resident(wout.shape), row(D_MODEL),
                  _resident((1, D_MODEL))],
        out_specs=[row(D_MODEL), row(D_MODEL)],
        out_shape=[jax.ShapeDtypeStruct((t, D_MODEL), F32),
                   jax.ShapeDtypeStruct((t, D_MODEL), BF16)],
        scratch_shapes=[pltpu.VMEM((4, SLOT_SLABS, tm, LANES), F32)],
        compiler_params=_params(1),
        name="mix",
    )(cact, wco, flat0(outs[0]), outs[1], outs[2], flat0(lses[0]), lses[1], lses[2], wao, gates,
      wout, x2, gffn)


_FF_CHUNK = 256
_FF_SLABS = _FF_CHUNK // LANES
_FF_SLOTS = 3


def _halo_specs(tm, width, n_rows):
    per = tm // HALO
    last = n_rows // HALO - 1
    prev = pl.BlockSpec((HALO, width), lambda i: (jnp.maximum(i * per - 1, 0), 0))
    cur = pl.BlockSpec((tm, width), lambda i: (i, 0))
    nxt = pl.BlockSpec((HALO, width), lambda i: (jnp.minimum((i + 1) * per, last), 0))
    return prev, cur, nxt


def _ffn_kernel(up_ref_prev, uc_ref, un_ref_next, h_ref, wup_ref, dww_ref, dwb_ref, wdn_ref,
                gfin_ref, y_ref, uext_ref, ab_ref, acc_ref, perm_ref, *, tm, tiles_per_seq):
    pos = pl.program_id(0) % tiles_per_seq
    uext_ref[0:HALO, :] = jnp.where(pos != 0, up_ref_prev[...], jnp.zeros_like(up_ref_prev))
    uext_ref[HALO:HALO + tm, :] = uc_ref[...]
    uext_ref[HALO + tm:, :] = jnp.where(pos != tiles_per_seq - 1, un_ref_next[...],
                                        jnp.zeros_like(un_ref_next))
    uext = uext_ref[...]
    n_chunks = D_FF // _FF_CHUNK
    half_rows = tm // 2

    def up_proj(c):
        for half in range(2):
            col0 = half * D_FF + c * _FF_CHUNK
            res = _dot(uext, wup_ref[:, col0:col0 + _FF_CHUNK])
            for sl in range(_FF_SLABS):
                ab_ref[c % _FF_SLOTS, half, sl] = res[:, sl * LANES:(sl + 1) * LANES]

    def conv3_even_odd(src_ref, col0):
        cols = slice(col0, col0 + LANES)
        w0, w1, w2 = (dww_ref[j:j + 1, cols] for j in range(3))
        bias = dwb_ref[:, cols]
        win = [src_ref[pl.ds(HALO - 1 + k, half_rows, stride=2), :] for k in range(4)]
        even = w0 * win[0] + w1 * win[1] + w2 * win[2] + bias
        odd = w0 * win[1] + w1 * win[2] + w2 * win[3] + bias
        return even, odd

    for c in range(_FF_SLOTS - 1):
        up_proj(c)
    for c in range(n_chunks):
        if c + _FF_SLOTS - 1 < n_chunks:
            up_proj(c + _FF_SLOTS - 1)
        ca = c * _FF_CHUNK
        evens, odds = [], []
        for sl in range(_FF_SLABS):
            a_e, a_o = conv3_even_odd(ab_ref.at[c % _FF_SLOTS, 0, sl], ca + sl * LANES)
            v_e, v_o = conv3_even_odd(ab_ref.at[c % _FF_SLOTS, 1, sl], D_FF + ca + sl * LANES)
            evens.append((_silu(a_e) * v_e).astype(BF16))
            odds.append((_silu(a_o) * v_o).astype(BF16))
        s = jnp.concatenate([jnp.concatenate(evens, axis=1), jnp.concatenate(odds, axis=1)], axis=0)
        part = _dot(s, wdn_ref[ca:ca + _FF_CHUNK, :])
        if c == 0:
            acc_ref[...] = part
        else:
            acc_ref[...] += part

    for sl in range(N_SLABS):
        cols = slice(sl * LANES, (sl + 1) * LANES)
        perm_ref[sl, pl.ds(0, half_rows, stride=2), :] = acc_ref[0:half_rows, cols]
        perm_ref[sl, pl.ds(1, half_rows, stride=2), :] = acc_ref[half_rows:tm, cols]
    h = h_ref[...] + jnp.concatenate([perm_ref[sl] for sl in range(N_SLABS)], axis=1)
    y_ref[...] = h * _rms_scale(h) * gfin_ref[...]


def _ffn(un, h, wup, dww, dwb, wdn, gfin, seq, tm):
    t = h.shape[0]
    row = lambda width: pl.BlockSpec((tm, width), lambda i: (i, 0))
    up, uc, unx = _halo_specs(tm, D_MODEL, t)
    return pl.pallas_call(
        functools.partial(_ffn_kernel, tm=tm, tiles_per_seq=seq // tm),
        grid=(t // tm,),
        in_specs=[up, uc, unx, row(D_MODEL), _resident(wup.shape), _resident(dww.shape),
                  _resident(dwb.shape), _resident(wdn.shape), _resident((1, D_MODEL))],
        out_specs=row(D_MODEL),
        out_shape=jax.ShapeDtypeStruct((t, D_MODEL), F32),
        scratch_shapes=[pltpu.VMEM((tm + 2 * HALO, D_MODEL), BF16),
                        pltpu.VMEM((_FF_SLOTS, 2, _FF_SLABS, tm + 2 * HALO, LANES), F32),
                        pltpu.VMEM((tm, D_MODEL), F32),
                        pltpu.VMEM((N_SLABS, tm, LANES), F32)],
        compiler_params=_params(1),
        name="ffn",
    )(un, un, un, h, wup, dww, dwb, wdn, gfin)


def _reorder_w_in(w):
    a, glu, q, k, v, gates = jnp.split(
        w, np.cumsum((CONV_CH, CONV_CH, ATTN_WIDTH, ATTN_WIDTH, ATTN_WIDTH)).tolist(), axis=1)
    q = q * (HEAD_DIM ** -0.5)
    grp = lambda m, g: m[:, g * SLOT_WIDTH:(g + 1) * SLOT_WIDTH]
    qkv = [jnp.concatenate([grp(q, g), grp(k, g), grp(v, g)], axis=1) for g in range(N_GROUPS)]
    return jnp.concatenate([a, glu, *qkv, gates], axis=1).astype(BF16)


def kernel(x, norm_mix_g, w_in, b_gate, conv_dw_w, conv_dw_b, conv_ln_g, conv_ln_b, w_conv_out,
           w_attn_out, w_out, norm_ffn_g, w_up, ffn_dw_w, ffn_dw_b, w_down, norm_final_g):
    batch, seq, d = x.shape
    assert w_in.shape[0] == 1, "single-layer block"
    x2 = x.reshape(batch * seq, d)
    cact, gates, qkv0, qkv1, qkv2 = _in_proj(
        x2, norm_mix_g, _reorder_w_in(w_in[0]), b_gate, conv_dw_w[0], conv_dw_b, conv_ln_g,
        conv_ln_b, batch, seq, tm=256)
    qkv0 = qkv0.reshape(batch, 1, seq, QKV_WIDTH)
    outs, lses = zip(*[_attention_group(qkv, g, queries_per_step=1024)
                       for g, qkv in enumerate((qkv0, qkv1, qkv2))])
    h2, un = _mix(cact, w_conv_out[0].astype(BF16), outs, lses, w_attn_out[0].astype(BF16), gates,
                  w_out[0].astype(BF16), x2, norm_ffn_g, seq, tm=512)
    y2 = _ffn(un, h2, w_up[0].astype(BF16), ffn_dw_w[0], ffn_dw_b, w_down[0].astype(BF16),
              norm_final_g[None], seq, tm=512)
    return y2.reshape(batch, seq, d)
```

```python
import functools

import jax
import jax.numpy as jnp
import numpy as np
from jax import lax
from jax.experimental import pallas as pl
from jax.experimental.pallas import tpu as pltpu

F32 = jnp.float32
BF16 = jnp.bfloat16

D_MODEL = 1024
CONV_CH = 1024
CONV_WIDTH = 31
ATTN_GROUPS = ((128, 1), (512, 4), (2048, 16))
N_GROUPS = len(ATTN_GROUPS)
HEADS_PER_GROUP = 8
HEAD_DIM = 64
N_ATTN_HEADS = HEADS_PER_GROUP * N_GROUPS
ATTN_WIDTH = N_ATTN_HEADS * HEAD_DIM
SLOT_WIDTH = HEADS_PER_GROUP * HEAD_DIM
QKV_WIDTH = 3 * SLOT_WIDTH
D_FF = 2816
RMS_EPS = 1e-6
LN_EPS = 1e-5
MASK_VALUE = -1e30
LOG2_E = float(np.log2(np.e))
LN_2 = float(np.log(2.0))

LANES = 128
N_SLABS = D_MODEL // LANES
SLOT_SLABS = SLOT_WIDTH // LANES
HALF_BAND = 64
Q_SUB = 128
K_WIN = Q_SUB + 2 * HALF_BAND
HALO = 16
CONV_HALF = CONV_WIDTH // 2
VMEM_LIMIT = 56 * 1024 * 1024

_OFF_A, _OFF_G, _OFF_Q = 0, CONV_CH, 2 * CONV_CH
_OFF_K, _OFF_V = _OFF_Q + ATTN_WIDTH, _OFF_Q + 2 * ATTN_WIDTH
_OFF_GATE = _OFF_Q + 3 * ATTN_WIDTH
_OFF_END = _OFF_GATE + 2 * D_MODEL

_SLOPES = (np.float32(2.0) ** (np.float32(-8.0) * np.arange(1, N_ATTN_HEADS + 1, dtype=np.float32)
                               / np.float32(N_ATTN_HEADS))).astype(np.float64)


def _sigmoid(x):
    return 0.5 * jnp.tanh(0.5 * x) + 0.5


def _silu(x):
    return x * _sigmoid(x)


def _rms_scale(xf):
    return lax.rsqrt(jnp.mean(xf * xf, axis=-1, keepdims=True) + RMS_EPS)


def _dot(a, b):
    return jnp.dot(a, b, preferred_element_type=F32)


def _resident(shape):
    return pl.BlockSpec(shape, lambda *_: (0,) * len(shape), pipeline_mode=pl.Buffered(1))


def _params(n_axes):
    return pltpu.CompilerParams(dimension_semantics=("arbitrary",) * n_axes,
                                vmem_limit_bytes=VMEM_LIMIT)


_CONV_ROWS = 64


def _in_proj_kernel(x_ref, g_ref, w_ref, bg_ref, dww_ref, dwb_ref, lng_ref, lnb_ref,
                    cact_ref, gate_ref, qkv0_ref, qkv1_ref, qkv2_ref,
                    us_ref, zs_ref, ys_ref, *, tm, tiles_per_seq):
    step = pl.program_id(0)
    starts_seq = step % tiles_per_seq == 0

    @pl.when(step == 0)
    def _init():
        zs_ref[...] = jnp.zeros_like(zs_ref)

    xf = x_ref[...]
    u = xf * _rms_scale(xf) * g_ref[...]
    ub = u.astype(BF16)

    def proj(lhs, lo, hi):
        return _dot(lhs, w_ref[:, lo:hi])

    z = proj(ub, _OFF_A, _OFF_G) * _sigmoid(proj(ub, _OFF_G, _OFF_Q))
    z_head = jnp.where(starts_seq, 0.0, z[0:HALO])
    for s in range(N_SLABS):
        zs_ref[s, HALO + tm:, :] = z_head[:, s * LANES:(s + 1) * LANES]

    first = HALO - CONV_HALF
    half_rows = _CONV_ROWS // 2

    def conv_slab(s):
        cols = slice(s * LANES, (s + 1) * LANES)
        tap = lambda j: dww_ref[j:j + 1, cols]
        bias = jnp.broadcast_to(dwb_ref[:, cols], (half_rows, LANES))
        for r0 in range(0, tm, _CONV_ROWS):
            acc_e, acc_o = bias, bias
            for k in range(CONV_WIDTH + 1):
                win = zs_ref[s, pl.ds(r0 + first + k, half_rows, stride=2), :]
                if k < CONV_WIDTH:
                    acc_e = acc_e + tap(k) * win
                if k > 0:
                    acc_o = acc_o + tap(k - 1) * win
            ys_ref[s, pl.ds(r0, half_rows, stride=2), :] = acc_e
            ys_ref[s, pl.ds(r0 + 1, half_rows, stride=2), :] = acc_o

    def qkv(lhs, group):
        part = lambda off: proj(lhs, off + group * SLOT_WIDTH, off + (group + 1) * SLOT_WIDTH)
        q = part(_OFF_Q) * (LOG2_E * HEAD_DIM ** -0.5)
        return jnp.concatenate([q, part(_OFF_K), part(_OFF_V)], axis=1).astype(BF16)

    for s in range(N_SLABS):
        conv_slab(s)
    gate_ref[...] = _sigmoid(proj(ub, _OFF_GATE, _OFF_END) + bg_ref[...]).astype(BF16)
    qkv0_ref[...] = qkv(ub, 0)

    for s in range(N_SLABS):
        us_ref[s] = u[:, s * LANES:(s + 1) * LANES]
    for out_ref, group in ((qkv1_ref, 1), (qkv2_ref, 2)):
        r = ATTN_GROUPS[group][1]
        n = tm // r
        lhs = jnp.concatenate(
            [jnp.concatenate([us_ref[s, pl.ds(c, n, stride=r), :] for s in range(N_SLABS)], axis=1)
             for c in range(r)], axis=0).astype(BF16)
        res = qkv(lhs, group)
        for c in range(r):
            out_ref[c] = res[c * n:(c + 1) * n]

    y = jnp.concatenate([ys_ref[s] for s in range(N_SLABS)], axis=1)
    mu = jnp.mean(y, axis=-1, keepdims=True)
    yc = y - mu
    var = jnp.mean(yc * yc, axis=-1, keepdims=True)
    yn = yc * lax.rsqrt(var + LN_EPS) * lng_ref[...] + lnb_ref[...]
    cact_ref[...] = _silu(yn).astype(BF16)

    for s in range(N_SLABS):
        tail = zs_ref[s, tm:tm + HALO, :]
        zs_ref[s, 0:HALO, :] = jnp.where(starts_seq, 0.0, tail)
        zs_ref[s, HALO:HALO + tm, :] = z[:, s * LANES:(s + 1) * LANES]


def _in_proj(x2, g, w_in, b_gate, dww, dwb, lng, lnb, batch, seq, tm):
    t = x2.shape[0]
    tiles = seq // tm
    n_tiles = t // tm
    cur = lambda i: jnp.minimum(i, n_tiles - 1)
    row = lambda width: pl.BlockSpec((tm, width), lambda i: (cur(i), 0))
    lagged = pl.BlockSpec((tm, CONV_CH), lambda i: (jnp.maximum(i - 1, 0), 0))

    def class_major(r):
        return pl.BlockSpec((None, r, tm // r, QKV_WIDTH),
                            lambda i: (cur(i) // tiles, 0, cur(i) % tiles, 0))

    r1, r2 = ATTN_GROUPS[1][1], ATTN_GROUPS[2][1]
    vec = _resident((1, CONV_CH))
    return pl.pallas_call(
        functools.partial(_in_proj_kernel, tm=tm, tiles_per_seq=tiles),
        grid=(n_tiles + 1,),
        in_specs=[row(D_MODEL), _resident((1, D_MODEL)), _resident(w_in.shape),
                  _resident((1, 2 * D_MODEL)), _resident(dww.shape), vec, vec, vec],
        out_specs=[lagged, row(2 * D_MODEL), row(QKV_WIDTH), class_major(r1), class_major(r2)],
        out_shape=[jax.ShapeDtypeStruct((t, CONV_CH), BF16),
                   jax.ShapeDtypeStruct((t, 2 * D_MODEL), BF16),
                   jax.ShapeDtypeStruct((t, QKV_WIDTH), BF16),
                   jax.ShapeDtypeStruct((batch, r1, seq // r1, QKV_WIDTH), BF16),
                   jax.ShapeDtypeStruct((batch, r2, seq // r2, QKV_WIDTH), BF16)],
        scratch_shapes=[pltpu.VMEM((N_SLABS, tm, LANES), F32),
                        pltpu.VMEM((N_SLABS, tm + 2 * HALO, LANES), F32),
                        pltpu.VMEM((N_SLABS, tm, LANES), F32)],
        compiler_params=_params(1),
        name="in_proj",
    )(x2, g, w_in, b_gate, dww, dwb, lng, lnb)


def _attn_kernel(q_ref, k_ref, v_ref, o_ref, lse_ref, bias_ref, *, n_cls, n_sub, length,
                 slope_dist):
    n_blocks = length // Q_SUB

    @pl.when((pl.program_id(0) == 0) & (pl.program_id(1) == 0) & (pl.program_id(2) == 0))
    def _fill_bias():
        row = lax.broadcasted_iota(jnp.int32, (Q_SUB, K_WIN), 0)
        col = lax.broadcasted_iota(jnp.int32, (Q_SUB, K_WIN), 1)
        for e in range(3):
            dist = jnp.abs(col - row - (2 - e) * HALF_BAND)
            distf = dist.astype(F32)
            for h in range(HEADS_PER_GROUP):
                bias_ref[e, h // 2, (h % 2) * Q_SUB:(h % 2 + 1) * Q_SUB, :] = jnp.where(
                    dist <= HALF_BAND, (-LOG2_E * slope_dist[h]) * distf, MASK_VALUE)

    lane = lax.broadcasted_iota(jnp.int32, (1, LANES), 1)
    low_half = lane < HEAD_DIM

    def sub_block(cls, j):
        blk = pl.program_id(2) * n_sub + j
        m0 = blk * Q_SUB
        start = pl.multiple_of(jnp.clip(m0 - HALF_BAND, 0, length - K_WIN), HALF_BAND)
        edge = jnp.where(blk == 0, 2, jnp.where(blk == n_blocks - 1, 0, 1))
        rows = slice(j * Q_SUB, (j + 1) * Q_SUB)
        lse = jnp.zeros((Q_SUB, LANES), F32)
        for pair in range(HEADS_PER_GROUP // 2):
            cols = slice(pair * LANES, (pair + 1) * LANES)
            qs = q_ref[cls, rows, cols]
            ks = k_ref[cls, pl.ds(start, K_WIN), cols]
            vs = v_ref[cls, pl.ds(start, K_WIN), cols]
            zero = jnp.zeros_like(qs)
            q2 = jnp.concatenate([jnp.where(low_half, qs, zero), jnp.where(low_half, zero, qs)], axis=0)
            s = lax.dot_general(q2, ks, (((1,), (1,)), ((), ())), preferred_element_type=F32)
            s = s + bias_ref[edge, pair]
            m = jnp.max(s, axis=-1, keepdims=True)
            p = jnp.exp2(s - m).astype(BF16)
            out = _dot(p, jnp.concatenate([vs, jnp.ones_like(vs)], axis=1))
            num = jnp.where(low_half, out[:Q_SUB, :LANES], out[Q_SUB:, :LANES])
            den = jnp.where(low_half, out[:Q_SUB, LANES:], out[Q_SUB:, LANES:])
            m2 = jnp.where(low_half, m[:Q_SUB], m[Q_SUB:])
            o_ref[cls, rows, cols] = (num / den).astype(BF16)
            lse = jnp.where((lane == pair) | (lane == HEAD_DIM + pair),
                            (m2 + jnp.log2(den)) * LN_2, lse)
        lse_ref[cls, rows, :] = lse

    for cls in range(n_cls):
        for j in range(n_sub):
            sub_block(cls, j)


def _attention_group(qkv, group, queries_per_step):
    batch, r, length, _ = qkv.shape
    bq = min(queries_per_step, length)
    n_cls = queries_per_step // bq
    slope_dist = tuple(float(_SLOPES[group * HEADS_PER_GROUP + h] * r) for h in range(HEADS_PER_GROUP))
    q_spec = pl.BlockSpec((None, n_cls, bq, SLOT_WIDTH), lambda b, c, i: (b, c, i, 0))
    k_spec = pl.BlockSpec((None, n_cls, length, SLOT_WIDTH), lambda b, c, i: (b, c, 0, 1))
    v_spec = pl.BlockSpec((None, n_cls, length, SLOT_WIDTH), lambda b, c, i: (b, c, 0, 2))
    out_spec = pl.BlockSpec((None, n_cls, bq, SLOT_WIDTH), lambda b, c, i: (b, c, i, 0))
    lse_spec = pl.BlockSpec((None, n_cls, bq, LANES), lambda b, c, i: (b, c, i, 0))
    return pl.pallas_call(
        functools.partial(_attn_kernel, n_cls=n_cls, n_sub=bq // Q_SUB, length=length,
                          slope_dist=slope_dist),
        grid=(batch, r // n_cls, length // bq),
        in_specs=[q_spec, k_spec, v_spec],
        out_specs=[out_spec, lse_spec],
        out_shape=[jax.ShapeDtypeStruct((batch, r, length, SLOT_WIDTH), BF16),
                   jax.ShapeDtypeStruct((batch, r, length, LANES), F32)],
        scratch_shapes=[pltpu.VMEM((3, HEADS_PER_GROUP // 2, 2 * Q_SUB, K_WIN), F32)],
        compiler_params=_params(3),
        name=f"attn_g{group}",
    )(qkv, qkv, qkv)


def _interleave(src_ref, dst_ref, r, tm):
    n = tm // r
    n_slabs = dst_ref.shape[0]
    for c in range(r):
        piece = src_ref[c].astype(F32)
        for s in range(n_slabs):
            dst_ref[s, pl.ds(c, n, stride=r), :] = piece[:, s * LANES:(s + 1) * LANES]
    return jnp.concatenate([dst_ref[s] for s in range(n_slabs)], axis=1)


def _head_expansion():
    row = lax.broadcasted_iota(jnp.int32, (LANES, SLOT_WIDTH), 0)
    col = lax.broadcasted_iota(jnp.int32, (LANES, SLOT_WIDTH), 1)
    shift = HEAD_DIM.bit_length() - 1
    pair, odd = row & (HEAD_DIM - 1), lax.shift_right_logical(row, shift)
    hit = (pair < HEADS_PER_GROUP // 2) & (lax.shift_right_logical(col, shift) == 2 * pair + odd)
    e = jnp.where(hit, 1.0, 0.0).astype(BF16)
    return jnp.concatenate([e, e], axis=0)


def _mix_kernel(cact_ref, wco_ref, o0_ref, o1_ref, o2_ref, l0_ref, l1_ref, l2_ref, wao_ref,
                gate_ref, wout_ref, x_ref, gffn_ref, h_ref, un_ref, il_ref, *, tm):
    conv_out = _dot(cact_ref[...], wco_ref[...])

    r1, r2 = ATTN_GROUPS[1][1], ATTN_GROUPS[2][1]
    l0 = l0_ref[...]
    l1 = _interleave(l1_ref, il_ref.at[0:1], r1, tm)
    l2 = _interleave(l2_ref, il_ref.at[1:2], r2, tm)
    mx = jnp.maximum(jnp.maximum(l0, l1), l2)
    e0, e1, e2 = jnp.exp(l0 - mx), jnp.exp(l1 - mx), jnp.exp(l2 - mx)
    inv = 1.0 / (e0 + e1 + e2)
    expansion = _head_expansion()

    def spread(w):
        hi = w.astype(BF16)
        lo = (w - hi.astype(F32)).astype(BF16)
        return _dot(jnp.concatenate([hi, lo], axis=1), expansion)

    o0 = o0_ref[...].astype(F32)
    o1 = _interleave(o1_ref, il_ref.at[2:2 + SLOT_SLABS], r1, tm)
    o2 = _interleave(o2_ref, il_ref.at[2 + SLOT_SLABS:2 + 2 * SLOT_SLABS], r2, tm)
    attn = spread(e0 * inv) * o0 + spread(e1 * inv) * o1 + spread(e2 * inv) * o2
    attn_out = _dot(attn.astype(BF16), wao_ref[...])

    merged = (gate_ref[:, :D_MODEL].astype(F32) * conv_out
              + gate_ref[:, D_MODEL:].astype(F32) * attn_out)
    h = x_ref[...] + _dot(merged.astype(BF16), wout_ref[...])
    h_ref[...] = h
    un_ref[...] = (h * _rms_scale(h) * gffn_ref[...]).astype(BF16)


def _mix(cact, wco, outs, lses, wao, gates, wout, x2, gffn, seq, tm):
    t = x2.shape[0]
    tiles = seq // tm
    row = lambda width: pl.BlockSpec((tm, width), lambda i: (i, 0))

    def class_major(r, width):
        return pl.BlockSpec((None, r, tm // r, width), lambda i: (i // tiles, 0, i % tiles, 0))

    def attn_specs(width):
        return [row(width)] + [class_major(ATTN_GROUPS[g][1], width) for g in (1, 2)]

    flat0 = lambda a: a.reshape(t, a.shape[-1])
    return pl.pallas_call(
        functools.partial(_mix_kernel, tm=tm),
        grid=(t // tm,),
        in_specs=[row(CONV_CH), _resident(wco.shape), *attn_specs(SLOT_WIDTH), *attn_specs(LANES),
                  _resident(wao.shape), row(2 * D_MODEL), _resident(wout.shape), row(D_MODEL),
                  _resident((1, D_MODEL))],
        out_specs=[row(D_MODEL), row(D_MODEL)],
        out_shape=[jax.ShapeDtypeStruct((t, D_MODEL), F32),
                   jax.ShapeDtypeStruct((t, D_MODEL), BF16)],
        scratch_shapes=[pltpu.VMEM((2 + 2 * SLOT_SLABS, tm, LANES), F32)],
        compiler_params=_params(1),
        name="mix",
    )(cact, wco, flat0(outs[0]), outs[1], outs[2], flat0(lses[0]), lses[1], lses[2], wao, gates,
      wout, x2, gffn)


_FF_CHUNK = 256
_FF_SLABS = _FF_CHUNK // LANES
_FF_SLOTS = 3


def _halo_specs(tm, width, n_rows):
    per = tm // HALO
    last = n_rows // HALO - 1
    prev = pl.BlockSpec((HALO, width), lambda i: (jnp.maximum(i * per - 1, 0), 0))
    cur = pl.BlockSpec((tm, width), lambda i: (i, 0))
    nxt = pl.BlockSpec((HALO, width), lambda i: (jnp.minimum((i + 1) * per, last), 0))
    return prev, cur, nxt


def _ffn_kernel(up_ref_prev, uc_ref, un_ref_next, h_ref, wup_ref, dww_ref, dwb_ref, wdn_ref,
                gfin_ref, y_ref, uext_ref, ab_ref, acc_ref, perm_ref, *, tm, tiles_per_seq):
    pos = pl.program_id(0) % tiles_per_seq
    uext_ref[0:HALO, :] = jnp.where(pos != 0, up_ref_prev[...], jnp.zeros_like(up_ref_prev))
    uext_ref[HALO:HALO + tm, :] = uc_ref[...]
    uext_ref[HALO + tm:, :] = jnp.where(pos != tiles_per_seq - 1, un_ref_next[...],
                                        jnp.zeros_like(un_ref_next))
    uext = uext_ref[...]
    n_chunks = D_FF // _FF_CHUNK
    half_rows = tm // 2

    def up_proj(c):
        for half in range(2):
            col0 = half * D_FF + c * _FF_CHUNK
            res = _dot(uext, wup_ref[:, col0:col0 + _FF_CHUNK])
            for sl in range(_FF_SLABS):
                ab_ref[c % _FF_SLOTS, half, sl] = res[:, sl * LANES:(sl + 1) * LANES]

    def conv3_even_odd(src_ref, col0):
        cols = slice(col0, col0 + LANES)
        w0, w1, w2 = (dww_ref[j:j + 1, cols] for j in range(3))
        bias = dwb_ref[:, cols]
        win = [src_ref[pl.ds(HALO - 1 + k, half_rows, stride=2), :] for k in range(4)]
        even = w0 * win[0] + w1 * win[1] + w2 * win[2] + bias
        odd = w0 * win[1] + w1 * win[2] + w2 * win[3] + bias
        return even, odd

    for c in range(_FF_SLOTS - 1):
        up_proj(c)
    for c in range(n_chunks):
        if c + _FF_SLOTS - 1 < n_chunks:
            up_proj(c + _FF_SLOTS - 1)
        ca = c * _FF_CHUNK
        evens, odds = [], []
        for sl in range(_FF_SLABS):
            a_e, a_o = conv3_even_odd(ab_ref.at[c % _FF_SLOTS, 0, sl], ca + sl * LANES)
            v_e, v_o = conv3_even_odd(ab_ref.at[c % _FF_SLOTS, 1, sl], D_FF + ca + sl * LANES)
            evens.append((_silu(a_e) * v_e).astype(BF16))
            odds.append((_silu(a_o) * v_o).astype(BF16))
        s = jnp.concatenate([jnp.concatenate(evens, axis=1), jnp.concatenate(odds, axis=1)], axis=0)
        part = _dot(s, wdn_ref[ca:ca + _FF_CHUNK, :])
        if c == 0:
            acc_ref[...] = part
        else:
            acc_ref[...] += part

    for sl in range(N_SLABS):
        cols = slice(sl * LANES, (sl + 1) * LANES)
        perm_ref[sl, pl.ds(0, half_rows, stride=2), :] = acc_ref[0:half_rows, cols]
        perm_ref[sl, pl.ds(1, half_rows, stride=2), :] = acc_ref[half_rows:tm, cols]
    h = h_ref[...] + jnp.concatenate([perm_ref[sl] for sl in range(N_SLABS)], axis=1)
    y_ref[...] = h * _rms_scale(h) * gfin_ref[...]


def _ffn(un, h, wup, dww, dwb, wdn, gfin, seq, tm):
    t = h.shape[0]
    row = lambda width: pl.BlockSpec((tm, width), lambda i: (i, 0))
    up, uc, unx = _halo_specs(tm, D_MODEL, t)
    return pl.pallas_call(
        functools.partial(_ffn_kernel, tm=tm, tiles_per_seq=seq // tm),
        grid=(t // tm,),
        in_specs=[up, uc, unx, row(D_MODEL), _resident(wup.shape), _resident(dww.shape),
                  _resident(dwb.shape), _resident(wdn.shape), _resident((1, D_MODEL))],
        out_specs=row(D_MODEL),
        out_shape=jax.ShapeDtypeStruct((t, D_MODEL), F32),
        scratch_shapes=[pltpu.VMEM((tm + 2 * HALO, D_MODEL), BF16),
                        pltpu.VMEM((_FF_SLOTS, 2, _FF_SLABS, tm + 2 * HALO, LANES), F32),
                        pltpu.VMEM((tm, D_MODEL), F32),
                        pltpu.VMEM((N_SLABS, tm, LANES), F32)],
        compiler_params=_params(1),
        name="ffn",
    )(un, un, un, h, wup, dww, dwb, wdn, gfin)


def kernel(x, norm_mix_g, w_in, b_gate, conv_dw_w, conv_dw_b, conv_ln_g, conv_ln_b, w_conv_out,
           w_attn_out, w_out, norm_ffn_g, w_up, ffn_dw_w, ffn_dw_b, w_down, norm_final_g):
    batch, seq, d = x.shape
    assert w_in.shape[0] == 1, "single-layer block"
    x2 = x.reshape(batch * seq, d)
    cact, gates, qkv0, qkv1, qkv2 = _in_proj(
        x2, norm_mix_g, w_in[0].astype(BF16), b_gate, conv_dw_w[0], conv_dw_b, conv_ln_g,
        conv_ln_b, batch, seq, tm=256)
    qkv0 = qkv0.reshape(batch, 1, seq, QKV_WIDTH)
    outs, lses = zip(*[_attention_group(qkv, g, queries_per_step=1024)
                       for g, qkv in enumerate((qkv0, qkv1, qkv2))])
    h2, un = _mix(cact, w_conv_out[0].astype(BF16), outs, lses, w_attn_out[0].astype(BF16), gates,
                  w_out[0].astype(BF16), x2, norm_ffn_g, seq, tm=512)
    y2 = _ffn(un, h2, w_up[0].astype(BF16), ffn_dw_w[0], ffn_dw_b, w_down[0].astype(BF16),
              norm_final_g[None], seq, tm=512)
    return y2.reshape(batch, seq, d)
```

```python
import functools

import jax
import jax.numpy as jnp
import numpy as np
from jax import lax
from jax.experimental import pallas as pl
from jax.experimental.pallas import tpu as pltpu

F32 = jnp.float32
BF16 = jnp.bfloat16

D_MODEL = 1024
CONV_CH = 1024
CONV_WIDTH = 31
ATTN_GROUPS = ((128, 1), (512, 4), (2048, 16))
N_GROUPS = len(ATTN_GROUPS)
HEADS_PER_GROUP = 8
HEAD_DIM = 64
N_ATTN_HEADS = HEADS_PER_GROUP * N_GROUPS
ATTN_WIDTH = N_ATTN_HEADS * HEAD_DIM
SLOT_WIDTH = HEADS_PER_GROUP * HEAD_DIM
QKV_WIDTH = 3 * SLOT_WIDTH
D_FF = 2816
RMS_EPS = 1e-6
LN_EPS = 1e-5
MASK_VALUE = -1e30
LOG2_E = float(np.log2(np.e))
LN_2 = float(np.log(2.0))

LANES = 128
N_SLABS = D_MODEL // LANES
SLOT_SLABS = SLOT_WIDTH // LANES
HALF_BAND = 64
Q_SUB = 128
K_WIN = Q_SUB + 2 * HALF_BAND
HALO = 16
CONV_HALF = CONV_WIDTH // 2
VMEM_LIMIT = 56 * 1024 * 1024

_OFF_A, _OFF_G, _OFF_Q = 0, CONV_CH, 2 * CONV_CH
_OFF_K, _OFF_V = _OFF_Q + ATTN_WIDTH, _OFF_Q + 2 * ATTN_WIDTH
_OFF_GATE = _OFF_Q + 3 * ATTN_WIDTH
_OFF_END = _OFF_GATE + 2 * D_MODEL

_SLOPES = (np.float32(2.0) ** (np.float32(-8.0) * np.arange(1, N_ATTN_HEADS + 1, dtype=np.float32)
                               / np.float32(N_ATTN_HEADS))).astype(np.float64)


def _sigmoid(x):
    return 0.5 * jnp.tanh(0.5 * x) + 0.5


def _silu(x):
    return x * _sigmoid(x)


def _rms_scale(xf):
    return lax.rsqrt(jnp.mean(xf * xf, axis=-1, keepdims=True) + RMS_EPS)


def _dot(a, b):
    return jnp.dot(a, b, preferred_element_type=F32)


def _resident(shape):
    return pl.BlockSpec(shape, lambda *_: (0,) * len(shape), pipeline_mode=pl.Buffered(1))


def _params(n_axes):
    return pltpu.CompilerParams(dimension_semantics=("arbitrary",) * n_axes,
                                vmem_limit_bytes=VMEM_LIMIT)


_CONV_ROWS = 64
_GLU_UNIT = 256
_PROJ_UNIT = 256


def _in_proj_kernel(x_ref, g_ref, w_ref, bg_ref, dww_ref, dwb_ref, lng_ref, lnb_ref,
                    cact_ref, gate_ref, qkv0_ref, qkv1_ref, qkv2_ref,
                    us_ref, zs_ref, ys_ref, zc_ref, *, tm, tiles_per_seq):
    step = pl.program_id(0)
    starts_seq = step % tiles_per_seq == 0

    @pl.when(step == 0)
    def _init():
        zs_ref[...] = jnp.zeros_like(zs_ref)

    xf = x_ref[...]
    u = xf * _rms_scale(xf) * g_ref[...]
    ub = u.astype(BF16)
    for s in range(N_SLABS):
        us_ref[s] = u[:, s * LANES:(s + 1) * LANES]

    def proj(lhs, lo, hi):
        return _dot(lhs, w_ref[:, lo:hi])

    def glu_unit(k):
        lo = k * _GLU_UNIT
        zk = proj(ub, _OFF_A + lo, _OFF_A + lo + _GLU_UNIT) * _sigmoid(
            proj(ub, _OFF_G + lo, _OFF_G + lo + _GLU_UNIT))
        for j in range(_GLU_UNIT // LANES):
            s = lo // LANES + j
            piece = zk[:, j * LANES:(j + 1) * LANES]
            zc_ref[s] = piece
            zs_ref[s, HALO + tm:, :] = jnp.where(starts_seq, 0.0, piece[0:HALO])

    first = HALO - CONV_HALF
    half_rows = _CONV_ROWS // 2

    def conv_unit(r0, s):
        cols = slice(s * LANES, (s + 1) * LANES)
        tap = lambda j: dww_ref[j:j + 1, cols]
        acc_e = acc_o = jnp.broadcast_to(dwb_ref[:, cols], (half_rows, LANES))
        for k in range(CONV_WIDTH + 1):
            win = zs_ref[s, pl.ds(r0 + first + k, half_rows, stride=2), :]
            if k < CONV_WIDTH:
                acc_e = acc_e + tap(k) * win
            if k > 0:
                acc_o = acc_o + tap(k - 1) * win
        ys_ref[s, pl.ds(r0, half_rows, stride=2), :] = acc_e
        ys_ref[s, pl.ds(r0 + 1, half_rows, stride=2), :] = acc_o

    def norm_unit(r0):
        rows = slice(r0, r0 + _CONV_ROWS)
        y = jnp.concatenate([ys_ref[s, rows, :] for s in range(N_SLABS)], axis=1)
        mu = jnp.mean(y, axis=-1, keepdims=True)
        yc = y - mu
        var = jnp.mean(yc * yc, axis=-1, keepdims=True)
        yn = yc * lax.rsqrt(var + LN_EPS) * lng_ref[...] + lnb_ref[...]
        cact_ref[rows, :] = _silu(yn).astype(BF16)

    def gate_unit(k):
        cols = slice(k * _PROJ_UNIT, (k + 1) * _PROJ_UNIT)
        logits = proj(ub, _OFF_GATE + cols.start, _OFF_GATE + cols.stop) + bg_ref[:, cols]
        gate_ref[:, cols] = _sigmoid(logits).astype(BF16)

    def class_major_lhs(r):
        n = tm // r
        return jnp.concatenate(
            [jnp.concatenate([us_ref[s, pl.ds(c, n, stride=r), :] for s in range(N_SLABS)], axis=1)
             for c in range(r)], axis=0).astype(BF16)

    def qkv_unit(out_ref, lhs, group, which, part):
        off = (_OFF_Q, _OFF_K, _OFF_V)[which] + group * SLOT_WIDTH + part * _PROJ_UNIT
        res = proj(lhs, off, off + _PROJ_UNIT)
        if which == 0:
            res = res * (LOG2_E * HEAD_DIM ** -0.5)
        res = res.astype(BF16)
        cols = slice(which * SLOT_WIDTH + part * _PROJ_UNIT,
                     which * SLOT_WIDTH + (part + 1) * _PROJ_UNIT)
        if group == 0:
            out_ref[:, cols] = res
        else:
            r = ATTN_GROUPS[group][1]
            for c in range(r):
                out_ref[c, :, cols] = res[c * (tm // r):(c + 1) * (tm // r)]

    @pl.when(step >= 0)
    def _interleaved():
        lhs = {0: ub, 1: class_major_lhs(ATTN_GROUPS[1][1]), 2: class_major_lhs(ATTN_GROUPS[2][1])}
        matmul_units = [functools.partial(glu_unit, k) for k in range(CONV_CH // _GLU_UNIT)]
        matmul_units += [functools.partial(gate_unit, k) for k in range(2 * D_MODEL // _PROJ_UNIT)]
        matmul_units += [functools.partial(qkv_unit, out_ref, lhs[group], group, which, part)
                         for group, out_ref in enumerate((qkv0_ref, qkv1_ref, qkv2_ref))
                         for which in range(3) for part in range(SLOT_WIDTH // _PROJ_UNIT)]
        vector_units = []
        for r0 in range(0, tm, _CONV_ROWS):
            vector_units += [functools.partial(conv_unit, r0, s) for s in range(N_SLABS)]
        done = 0
        for i, unit in enumerate(matmul_units):
            unit()
            due = -(-(i + 1) * len(vector_units) // len(matmul_units))
            for vec in vector_units[done:due]:
                vec()
            done = due
        assert done == len(vector_units)

    for r0 in range(0, tm, _CONV_ROWS):
        norm_unit(r0)

    for s in range(N_SLABS):
        tail = zs_ref[s, tm:tm + HALO, :]
        zs_ref[s, 0:HALO, :] = jnp.where(starts_seq, 0.0, tail)
        zs_ref[s, HALO:HALO + tm, :] = zc_ref[s]


def _in_proj(x2, g, w_in, b_gate, dww, dwb, lng, lnb, batch, seq, tm):
    t = x2.shape[0]
    tiles = seq // tm
    n_tiles = t // tm
    cur = lambda i: jnp.minimum(i, n_tiles - 1)
    row = lambda width: pl.BlockSpec((tm, width), lambda i: (cur(i), 0))
    lagged = pl.BlockSpec((tm, CONV_CH), lambda i: (jnp.maximum(i - 1, 0), 0))

    def class_major(r):
        return pl.BlockSpec((None, r, tm // r, QKV_WIDTH),
                            lambda i: (cur(i) // tiles, 0, cur(i) % tiles, 0))

    r1, r2 = ATTN_GROUPS[1][1], ATTN_GROUPS[2][1]
    vec = _resident((1, CONV_CH))
    return pl.pallas_call(
        functools.partial(_in_proj_kernel, tm=tm, tiles_per_seq=tiles),
        grid=(n_tiles + 1,),
        in_specs=[row(D_MODEL), _resident((1, D_MODEL)), _resident(w_in.shape),
                  _resident((1, 2 * D_MODEL)), _resident(dww.shape), vec, vec, vec],
        out_specs=[lagged, row(2 * D_MODEL), row(QKV_WIDTH), class_major(r1), class_major(r2)],
        out_shape=[jax.ShapeDtypeStruct((t, CONV_CH), BF16),
                   jax.ShapeDtypeStruct((t, 2 * D_MODEL), BF16),
                   jax.ShapeDtypeStruct((t, QKV_WIDTH), BF16),
                   jax.ShapeDtypeStruct((batch, r1, seq // r1, QKV_WIDTH), BF16),
                   jax.ShapeDtypeStruct((batch, r2, seq // r2, QKV_WIDTH), BF16)],
        scratch_shapes=[pltpu.VMEM((N_SLABS, tm, LANES), F32),
                        pltpu.VMEM((N_SLABS, tm + 2 * HALO, LANES), F32),
                        pltpu.VMEM((N_SLABS, tm, LANES), F32),
                        pltpu.VMEM((N_SLABS, tm, LANES), F32)],
        compiler_params=_params(1),
        name="in_proj",
    )(x2, g, w_in, b_gate, dww, dwb, lng, lnb)


def _attn_kernel(q_ref, k_ref, v_ref, o_ref, lse_ref, bias_ref, *, n_cls, n_sub, length,
                 slope_dist):
    n_blocks = length // Q_SUB

    @pl.when((pl.program_id(0) == 0) & (pl.program_id(1) == 0) & (pl.program_id(2) == 0))
    def _fill_bias():
        row = lax.broadcasted_iota(jnp.int32, (Q_SUB, K_WIN), 0)
        col = lax.broadcasted_iota(jnp.int32, (Q_SUB, K_WIN), 1)
        for e in range(3):
            dist = jnp.abs(col - row - (2 - e) * HALF_BAND)
            distf = dist.astype(F32)
            for h in range(HEADS_PER_GROUP):
                bias_ref[e, h // 2, (h % 2) * Q_SUB:(h % 2 + 1) * Q_SUB, :] = jnp.where(
                    dist <= HALF_BAND, (-LOG2_E * slope_dist[h]) * distf, MASK_VALUE)

    lane = lax.broadcasted_iota(jnp.int32, (1, LANES), 1)
    low_half = lane < HEAD_DIM

    def sub_block(cls, j):
        blk = pl.program_id(2) * n_sub + j
        m0 = blk * Q_SUB
        start = pl.multiple_of(jnp.clip(m0 - HALF_BAND, 0, length - K_WIN), HALF_BAND)
        edge = jnp.where(blk == 0, 2, jnp.where(blk == n_blocks - 1, 0, 1))
        rows = slice(j * Q_SUB, (j + 1) * Q_SUB)
        lse = jnp.zeros((Q_SUB, LANES), F32)
        for pair in range(HEADS_PER_GROUP // 2):
            cols = slice(pair * LANES, (pair + 1) * LANES)
            qs = q_ref[cls, rows, cols]
            ks = k_ref[cls, pl.ds(start, K_WIN), cols]
            vs = v_ref[cls, pl.ds(start, K_WIN), cols]
            zero = jnp.zeros_like(qs)
            q2 = jnp.concatenate([jnp.where(low_half, qs, zero), jnp.where(low_half, zero, qs)], axis=0)
            s = lax.dot_general(q2, ks, (((1,), (1,)), ((), ())), preferred_element_type=F32)
            s = s + bias_ref[edge, pair]
            m = jnp.max(s, axis=-1, keepdims=True)
            p = jnp.exp2(s - m).astype(BF16)
            out = _dot(p, jnp.concatenate([vs, jnp.ones_like(vs)], axis=1))
            num = jnp.where(low_half, out[:Q_SUB, :LANES], out[Q_SUB:, :LANES])
            den = jnp.where(low_half, out[:Q_SUB, LANES:], out[Q_SUB:, LANES:])
            m2 = jnp.where(low_half, m[:Q_SUB], m[Q_SUB:])
            o_ref[cls, rows, cols] = (num / den).astype(BF16)
            lse = jnp.where((lane == pair) | (lane == HEAD_DIM + pair),
                            (m2 + jnp.log2(den)) * LN_2, lse)
        lse_ref[cls, rows, :] = lse

    for cls in range(n_cls):
        for j in range(n_sub):
            sub_block(cls, j)


def _attention_group(qkv, group, queries_per_step):
    batch, r, length, _ = qkv.shape
    bq = min(queries_per_step, length)
    n_cls = queries_per_step // bq
    slope_dist = tuple(float(_SLOPES[group * HEADS_PER_GROUP + h] * r) for h in range(HEADS_PER_GROUP))
    q_spec = pl.BlockSpec((None, n_cls, bq, SLOT_WIDTH), lambda b, c, i: (b, c, i, 0))
    k_spec = pl.BlockSpec((None, n_cls, length, SLOT_WIDTH), lambda b, c, i: (b, c, 0, 1))
    v_spec = pl.BlockSpec((None, n_cls, length, SLOT_WIDTH), lambda b, c, i: (b, c, 0, 2))
    out_spec = pl.BlockSpec((None, n_cls, bq, SLOT_WIDTH), lambda b, c, i: (b, c, i, 0))
    lse_spec = pl.BlockSpec((None, n_cls, bq, LANES), lambda b, c, i: (b, c, i, 0))
    return pl.pallas_call(
        functools.partial(_attn_kernel, n_cls=n_cls, n_sub=bq // Q_SUB, length=length,
                          slope_dist=slope_dist),
        grid=(batch, r // n_cls, length // bq),
        in_specs=[q_spec, k_spec, v_spec],
        out_specs=[out_spec, lse_spec],
        out_shape=[jax.ShapeDtypeStruct((batch, r, length, SLOT_WIDTH), BF16),
                   jax.ShapeDtypeStruct((batch, r, length, LANES), F32)],
        scratch_shapes=[pltpu.VMEM((3, HEADS_PER_GROUP // 2, 2 * Q_SUB, K_WIN), F32)],
        compiler_params=_params(3),
        name=f"attn_g{group}",
    )(qkv, qkv, qkv)


def _interleave(src_ref, dst_ref, r, tm):
    n = tm // r
    n_slabs = dst_ref.shape[0]
    for c in range(r):
        piece = src_ref[c].astype(F32)
        for s in range(n_slabs):
            dst_ref[s, pl.ds(c, n, stride=r), :] = piece[:, s * LANES:(s + 1) * LANES]
    return jnp.concatenate([dst_ref[s] for s in range(n_slabs)], axis=1)


def _head_expansion():
    row = lax.broadcasted_iota(jnp.int32, (LANES, SLOT_WIDTH), 0)
    col = lax.broadcasted_iota(jnp.int32, (LANES, SLOT_WIDTH), 1)
    shift = HEAD_DIM.bit_length() - 1
    pair, odd = row & (HEAD_DIM - 1), lax.shift_right_logical(row, shift)
    hit = (pair < HEADS_PER_GROUP // 2) & (lax.shift_right_logical(col, shift) == 2 * pair + odd)
    e = jnp.where(hit, 1.0, 0.0).astype(BF16)
    return jnp.concatenate([e, e], axis=0)


def _mix_kernel(cact_ref, wco_ref, o0_ref, o1_ref, o2_ref, l0_ref, l1_ref, l2_ref, wao_ref,
                gate_ref, wout_ref, x_ref, gffn_ref, h_ref, un_ref, il_ref, *, tm):
    conv_out = _dot(cact_ref[...], wco_ref[...])

    r1, r2 = ATTN_GROUPS[1][1], ATTN_GROUPS[2][1]
    l0 = l0_ref[...]
    l1 = _interleave(l1_ref, il_ref.at[0:1], r1, tm)
    l2 = _interleave(l2_ref, il_ref.at[1:2], r2, tm)
    mx = jnp.maximum(jnp.maximum(l0, l1), l2)
    e0, e1, e2 = jnp.exp(l0 - mx), jnp.exp(l1 - mx), jnp.exp(l2 - mx)
    inv = 1.0 / (e0 + e1 + e2)
    expansion = _head_expansion()

    def spread(w):
        hi = w.astype(BF16)
        lo = (w - hi.astype(F32)).astype(BF16)
        return _dot(jnp.concatenate([hi, lo], axis=1), expansion)

    o0 = o0_ref[...].astype(F32)
    o1 = _interleave(o1_ref, il_ref.at[2:2 + SLOT_SLABS], r1, tm)
    o2 = _interleave(o2_ref, il_ref.at[2 + SLOT_SLABS:2 + 2 * SLOT_SLABS], r2, tm)
    attn = spread(e0 * inv) * o0 + spread(e1 * inv) * o1 + spread(e2 * inv) * o2
    attn_out = _dot(attn.astype(BF16), wao_ref[...])

    merged = (gate_ref[:, :D_MODEL].astype(F32) * conv_out
              + gate_ref[:, D_MODEL:].astype(F32) * attn_out)
    h = x_ref[...] + _dot(merged.astype(BF16), wout_ref[...])
    h_ref[...] = h
    un_ref[...] = (h * _rms_scale(h) * gffn_ref[...]).astype(BF16)


def _mix(cact, wco, outs, lses, wao, gates, wout, x2, gffn, seq, tm):
    t = x2.shape[0]
    tiles = seq // tm
    row = lambda width: pl.BlockSpec((tm, width), lambda i: (i, 0))

    def class_major(r, width):
        return pl.BlockSpec((None, r, tm // r, width), lambda i: (i // tiles, 0, i % tiles, 0))

    def attn_specs(width):
        return [row(width)] + [class_major(ATTN_GROUPS[g][1], width) for g in (1, 2)]

    flat0 = lambda a: a.reshape(t, a.shape[-1])
    return pl.pallas_call(
        functools.partial(_mix_kernel, tm=tm),
        grid=(t // tm,),
        in_specs=[row(CONV_CH), _resident(wco.shape), *attn_specs(SLOT_WIDTH), *attn_specs(LANES),
                  _resident(wao.shape), row(2 * D_MODEL), _resident(wout.shape), row(D_MODEL),
                  _resident((1, D_MODEL))],
        out_specs=[row(D_MODEL), row(D_MODEL)],
        out_shape=[jax.ShapeDtypeStruct((t, D_MODEL), F32),
                   jax.ShapeDtypeStruct((t, D_MODEL), BF16)],
        scratch_shapes=[pltpu.VMEM((2 + 2 * SLOT_SLABS, tm, LANES), F32)],
        compiler_params=_params(1),
        name="mix",
    )(cact, wco, flat0(outs[0]), outs[1], outs[2], flat0(lses[0]), lses[1], lses[2], wao, gates,
      wout, x2, gffn)


_FF_CHUNK = 256
_FF_SLABS = _FF_CHUNK // LANES
_FF_SLOTS = 3


def _halo_specs(tm, width, n_rows):
    per = tm // HALO
    last = n_rows // HALO - 1
    prev = pl.BlockSpec((HALO, width), lambda i: (jnp.maximum(i * per - 1, 0), 0))
    cur = pl.BlockSpec((tm, width), lambda i: (i, 0))
    nxt = pl.BlockSpec((HALO, width), lambda i: (jnp.minimum((i + 1) * per, last), 0))
    return prev, cur, nxt


def _ffn_kernel(up_ref_prev, uc_ref, un_ref_next, h_ref, wup_ref, dww_ref, dwb_ref, wdn_ref,
                gfin_ref, y_ref, uext_ref, ab_ref, acc_ref, perm_ref, *, tm, tiles_per_seq):
    pos = pl.program_id(0) % tiles_per_seq
    uext_ref[0:HALO, :] = jnp.where(pos != 0, up_ref_prev[...], jnp.zeros_like(up_ref_prev))
    uext_ref[HALO:HALO + tm, :] = uc_ref[...]
    uext_ref[HALO + tm:, :] = jnp.where(pos != tiles_per_seq - 1, un_ref_next[...],
                                        jnp.zeros_like(un_ref_next))
    uext = uext_ref[...]
    n_chunks = D_FF // _FF_CHUNK
    half_rows = tm // 2

    def up_proj(c):
        for half in range(2):
            col0 = half * D_FF + c * _FF_CHUNK
            res = _dot(uext, wup_ref[:, col0:col0 + _FF_CHUNK])
            for sl in range(_FF_SLABS):
                ab_ref[c % _FF_SLOTS, half, sl] = res[:, sl * LANES:(sl + 1) * LANES]

    def conv3_even_odd(src_ref, col0):
        cols = slice(col0, col0 + LANES)
        w0, w1, w2 = (dww_ref[j:j + 1, cols] for j in range(3))
        bias = dwb_ref[:, cols]
        win = [src_ref[pl.ds(HALO - 1 + k, half_rows, stride=2), :] for k in range(4)]
        even = w0 * win[0] + w1 * win[1] + w2 * win[2] + bias
        odd = w0 * win[1] + w1 * win[2] + w2 * win[3] + bias
        return even, odd

    for c in range(_FF_SLOTS - 1):
        up_proj(c)
    for c in range(n_chunks):
        if c + _FF_SLOTS - 1 < n_chunks:
            up_proj(c + _FF_SLOTS - 1)
        ca = c * _FF_CHUNK
        evens, odds = [], []
        for sl in range(_FF_SLABS):
            a_e, a_o = conv3_even_odd(ab_ref.at[c % _FF_SLOTS, 0, sl], ca + sl * LANES)
            v_e, v_o = conv3_even_odd(ab_ref.at[c % _FF_SLOTS, 1, sl], D_FF + ca + sl * LANES)
            evens.append((_silu(a_e) * v_e).astype(BF16))
            odds.append((_silu(a_o) * v_o).astype(BF16))
        s = jnp.concatenate([jnp.concatenate(evens, axis=1), jnp.concatenate(odds, axis=1)], axis=0)
        part = _dot(s, wdn_ref[ca:ca + _FF_CHUNK, :])
        if c == 0:
            acc_ref[...] = part
        else:
            acc_ref[...] += part

    for sl in range(N_SLABS):
        cols = slice(sl * LANES, (sl + 1) * LANES)
        perm_ref[sl, pl.ds(0, half_rows, stride=2), :] = acc_ref[0:half_rows, cols]
        perm_ref[sl, pl.ds(1, half_rows, stride=2), :] = acc_ref[half_rows:tm, cols]
    h = h_ref[...] + jnp.concatenate([perm_ref[sl] for sl in range(N_SLABS)], axis=1)
    y_ref[...] = h * _rms_scale(h) * gfin_ref[...]


def _ffn(un, h, wup, dww, dwb, wdn, gfin, seq, tm):
    t = h.shape[0]
    row = lambda width: pl.BlockSpec((tm, width), lambda i: (i, 0))
    up, uc, unx = _halo_specs(tm, D_MODEL, t)
    return pl.pallas_call(
        functools.partial(_ffn_kernel, tm=tm, tiles_per_seq=seq // tm),
        grid=(t // tm,),
        in_specs=[up, uc, unx, row(D_MODEL), _resident(wup.shape), _resident(dww.shape),
                  _resident(dwb.shape), _resident(wdn.shape), _resident((1, D_MODEL))],
        out_specs=row(D_MODEL),
        out_shape=jax.ShapeDtypeStruct((t, D_MODEL), F32),
        scratch_shapes=[pltpu.VMEM((tm + 2 * HALO, D_MODEL), BF16),
                        pltpu.VMEM((_FF_SLOTS, 2, _FF_SLABS, tm + 2 * HALO, LANES), F32),
                        pltpu.VMEM((tm, D_MODEL), F32),
                        pltpu.VMEM((N_SLABS, tm, LANES), F32)],
        compiler_params=_params(1),
        name="ffn",
    )(un, un, un, h, wup, dww, dwb, wdn, gfin)


def kernel(x, norm_mix_g, w_in, b_gate, conv_dw_w, conv_dw_b, conv_ln_g, conv_ln_b, w_conv_out,
           w_attn_out, w_out, norm_ffn_g, w_up, ffn_dw_w, ffn_dw_b, w_down, norm_final_g):
    batch, seq, d = x.shape
    assert w_in.shape[0] == 1, "single-layer block"
    x2 = x.reshape(batch * seq, d)
    cact, gates, qkv0, qkv1, qkv2 = _in_proj(
        x2, norm_mix_g, w_in[0].astype(BF16), b_gate, conv_dw_w[0], conv_dw_b, conv_ln_g,
        conv_ln_b, batch, seq, tm=256)
    qkv0 = qkv0.reshape(batch, 1, seq, QKV_WIDTH)
    outs, lses = zip(*[_attention_group(qkv, g, queries_per_step=1024)
                       for g, qkv in enumerate((qkv0, qkv1, qkv2))])
    h2, un = _mix(cact, w_conv_out[0].astype(BF16), outs, lses, w_attn_out[0].astype(BF16), gates,
                  w_out[0].astype(BF16), x2, norm_ffn_g, seq, tm=512)
    y2 = _ffn(un, h2, w_up[0].astype(BF16), ffn_dw_w[0], ffn_dw_b, w_down[0].astype(BF16),
              norm_final_g[None], seq, tm=512)
    return y2.reshape(batch, seq, d)
```

```python
import functools

import jax
import jax.numpy as jnp
import numpy as np
from jax import lax
from jax.experimental import pallas as pl
from jax.experimental.pallas import tpu as pltpu

F32 = jnp.float32
BF16 = jnp.bfloat16

D_MODEL = 1024
CONV_CH = 1024
CONV_WIDTH = 31
ATTN_GROUPS = ((128, 1), (512, 4), (2048, 16))
N_GROUPS = len(ATTN_GROUPS)
HEADS_PER_GROUP = 8
HEAD_DIM = 64
N_ATTN_HEADS = HEADS_PER_GROUP * N_GROUPS
ATTN_WIDTH = N_ATTN_HEADS * HEAD_DIM
SLOT_WIDTH = HEADS_PER_GROUP * HEAD_DIM
QKV_WIDTH = 3 * SLOT_WIDTH
D_FF = 2816
RMS_EPS = 1e-6
LN_EPS = 1e-5
MASK_VALUE = -1e30
LOG2_E = float(np.log2(np.e))
LN_2 = float(np.log(2.0))

LANES = 128
SUBLANES = 8
N_SLABS = D_MODEL // LANES
SLOT_SLABS = SLOT_WIDTH // LANES
HALF_BAND = 64
Q_SUB = 128
K_WIN = Q_SUB + 2 * HALF_BAND
HALO = 16
CONV_HALF = CONV_WIDTH // 2
VMEM_LIMIT = 56 * 1024 * 1024

_OFF_A, _OFF_G, _OFF_Q = 0, CONV_CH, 2 * CONV_CH
_OFF_K, _OFF_V = _OFF_Q + ATTN_WIDTH, _OFF_Q + 2 * ATTN_WIDTH
_OFF_GATE = _OFF_Q + 3 * ATTN_WIDTH
_OFF_END = _OFF_GATE + 2 * D_MODEL

_SLOPES = (np.float32(2.0) ** (np.float32(-8.0) * np.arange(1, N_ATTN_HEADS + 1, dtype=np.float32)
                               / np.float32(N_ATTN_HEADS))).astype(np.float64)


def _sigmoid(x):
    return 0.5 * jnp.tanh(0.5 * x) + 0.5


def _silu(x):
    return x * _sigmoid(x)


def _rms_scale(xf):
    return lax.rsqrt(jnp.mean(xf * xf, axis=-1, keepdims=True) + RMS_EPS)


def _dot(a, b):
    return jnp.dot(a, b, preferred_element_type=F32)


def _resident(shape):
    return pl.BlockSpec(shape, lambda *_: (0,) * len(shape), pipeline_mode=pl.Buffered(1))


def _params(n_axes):
    return pltpu.CompilerParams(dimension_semantics=("arbitrary",) * n_axes,
                                vmem_limit_bytes=VMEM_LIMIT)


_CONV_ROWS = 32
_GLU_UNIT = 256
_PROJ_UNIT = 256


def _in_proj_kernel(x_ref, g_ref, w_ref, bg_ref, dww_ref, dwb_ref, lng_ref, lnb_ref,
                    cact_ref, gate_ref, qkv0_ref, qkv1_ref, qkv2_ref,
                    us_ref, zs_ref, ys_ref, zc_ref, *, tm, tiles_per_seq):
    step = pl.program_id(0)
    starts_seq = step % tiles_per_seq == 0

    @pl.when(step == 0)
    def _init():
        zs_ref[...] = jnp.zeros_like(zs_ref)

    xf = x_ref[...]
    u = xf * _rms_scale(xf) * g_ref[...]
    ub = u.astype(BF16)
    for s in range(N_SLABS):
        us_ref[s] = u[:, s * LANES:(s + 1) * LANES]

    def proj(lhs, lo, hi):
        return _dot(lhs, w_ref[:, lo:hi])

    def glu_unit(k):
        lo = k * _GLU_UNIT
        zk = proj(ub, _OFF_A + lo, _OFF_A + lo + _GLU_UNIT) * _sigmoid(
            proj(ub, _OFF_G + lo, _OFF_G + lo + _GLU_UNIT))
        for j in range(_GLU_UNIT // LANES):
            s = lo // LANES + j
            piece = zk[:, j * LANES:(j + 1) * LANES]
            zc_ref[s] = piece
            zs_ref[s, HALO + tm:, :] = jnp.where(starts_seq, 0.0, piece[0:HALO])

    first = HALO - CONV_HALF
    half_rows = _CONV_ROWS // 2

    def conv_unit(r0, s):
        cols = slice(s * LANES, (s + 1) * LANES)
        tap = lambda j: dww_ref[j:j + 1, cols]
        acc_e = acc_o = jnp.broadcast_to(dwb_ref[:, cols], (half_rows, LANES))
        for k in range(CONV_WIDTH + 1):
            win = zs_ref[s, pl.ds(r0 + first + k, half_rows, stride=2), :]
            if k < CONV_WIDTH:
                acc_e = acc_e + tap(k) * win
            if k > 0:
                acc_o = acc_o + tap(k - 1) * win
        ys_ref[s, pl.ds(r0, half_rows, stride=2), :] = acc_e
        ys_ref[s, pl.ds(r0 + 1, half_rows, stride=2), :] = acc_o

    def norm_unit(r0):
        rows = slice(r0, r0 + _CONV_ROWS)
        y = jnp.concatenate([ys_ref[s, rows, :] for s in range(N_SLABS)], axis=1)
        mu = jnp.mean(y, axis=-1, keepdims=True)
        yc = y - mu
        var = jnp.mean(yc * yc, axis=-1, keepdims=True)
        yn = yc * lax.rsqrt(var + LN_EPS) * lng_ref[...] + lnb_ref[...]
        cact_ref[rows, :] = _silu(yn).astype(BF16)

    def gate_unit(k):
        cols = slice(k * _PROJ_UNIT, (k + 1) * _PROJ_UNIT)
        logits = proj(ub, _OFF_GATE + cols.start, _OFF_GATE + cols.stop) + bg_ref[:, cols]
        gate_ref[:, cols] = _sigmoid(logits).astype(BF16)

    def class_major_lhs(r):
        n = tm // r
        return jnp.concatenate(
            [jnp.concatenate([us_ref[s, pl.ds(c, n, stride=r), :] for s in range(N_SLABS)], axis=1)
             for c in range(r)], axis=0).astype(BF16)

    def qkv_unit(out_ref, lhs, group, which, part):
        off = (_OFF_Q, _OFF_K, _OFF_V)[which] + group * SLOT_WIDTH + part * _PROJ_UNIT
        res = proj(lhs, off, off + _PROJ_UNIT)
        if which == 0:
            res = res * (LOG2_E * HEAD_DIM ** -0.5)
        res = res.astype(BF16)
        cols = slice(which * SLOT_WIDTH + part * _PROJ_UNIT,
                     which * SLOT_WIDTH + (part + 1) * _PROJ_UNIT)
        if group == 0:
            out_ref[:, cols] = res
        else:
            r = ATTN_GROUPS[group][1]
            for c in range(r):
                out_ref[c, :, cols] = res[c * (tm // r):(c + 1) * (tm // r)]

    @pl.when(step >= 0)
    def _interleaved():
        lhs = {0: ub, 1: class_major_lhs(ATTN_GROUPS[1][1]), 2: class_major_lhs(ATTN_GROUPS[2][1])}
        matmul_units = [functools.partial(glu_unit, k) for k in range(CONV_CH // _GLU_UNIT)]
        matmul_units += [functools.partial(gate_unit, k) for k in range(2 * D_MODEL // _PROJ_UNIT)]
        matmul_units += [functools.partial(qkv_unit, out_ref, lhs[group], group, which, part)
                         for group, out_ref in enumerate((qkv0_ref, qkv1_ref, qkv2_ref))
                         for which in range(3) for part in range(SLOT_WIDTH // _PROJ_UNIT)]
        vector_units = []
        for r0 in range(0, tm, _CONV_ROWS):
            vector_units += [functools.partial(conv_unit, r0, s) for s in range(N_SLABS)]
        done = 0
        for i, unit in enumerate(matmul_units):
            unit()
            due = -(-(i + 1) * len(vector_units) // len(matmul_units))
            for vec in vector_units[done:due]:
                vec()
            done = due
        assert done == len(vector_units)

    for r0 in range(0, tm, _CONV_ROWS):
        norm_unit(r0)

    for s in range(N_SLABS):
        tail = zs_ref[s, tm:tm + HALO, :]
        zs_ref[s, 0:HALO, :] = jnp.where(starts_seq, 0.0, tail)
        zs_ref[s, HALO:HALO + tm, :] = zc_ref[s]


def _in_proj(x2, g, w_in, b_gate, dww, dwb, lng, lnb, batch, seq, tm):
    t = x2.shape[0]
    tiles = seq // tm
    n_tiles = t // tm
    cur = lambda i: jnp.minimum(i, n_tiles - 1)
    row = lambda width: pl.BlockSpec((tm, width), lambda i: (cur(i), 0))
    lagged = pl.BlockSpec((tm, CONV_CH), lambda i: (jnp.maximum(i - 1, 0), 0))

    def class_major(r):
        return pl.BlockSpec((None, r, tm // r, QKV_WIDTH),
                            lambda i: (cur(i) // tiles, 0, cur(i) % tiles, 0))

    r1, r2 = ATTN_GROUPS[1][1], ATTN_GROUPS[2][1]
    vec = _resident((1, CONV_CH))
    return pl.pallas_call(
        functools.partial(_in_proj_kernel, tm=tm, tiles_per_seq=tiles),
        grid=(n_tiles + 1,),
        in_specs=[row(D_MODEL), _resident((1, D_MODEL)), _resident(w_in.shape),
                  _resident((1, 2 * D_MODEL)), _resident(dww.shape), vec, vec, vec],
        out_specs=[lagged, row(2 * D_MODEL), row(QKV_WIDTH), class_major(r1), class_major(r2)],
        out_shape=[jax.ShapeDtypeStruct((t, CONV_CH), BF16),
                   jax.ShapeDtypeStruct((t, 2 * D_MODEL), BF16),
                   jax.ShapeDtypeStruct((t, QKV_WIDTH), BF16),
                   jax.ShapeDtypeStruct((batch, r1, seq // r1, QKV_WIDTH), BF16),
                   jax.ShapeDtypeStruct((batch, r2, seq // r2, QKV_WIDTH), BF16)],
        scratch_shapes=[pltpu.VMEM((N_SLABS, tm, LANES), F32),
                        pltpu.VMEM((N_SLABS, tm + 2 * HALO, LANES), F32),
                        pltpu.VMEM((N_SLABS, tm, LANES), F32),
                        pltpu.VMEM((N_SLABS, tm, LANES), F32)],
        compiler_params=_params(1),
        name="in_proj",
    )(x2, g, w_in, b_gate, dww, dwb, lng, lnb)


def _attn_kernel(q_ref, k_ref, v_ref, o_ref, lse_ref, bias_ref, *, n_cls, n_sub, length,
                 slope_dist):
    n_blocks = length // Q_SUB

    @pl.when((pl.program_id(0) == 0) & (pl.program_id(1) == 0) & (pl.program_id(2) == 0))
    def _fill_bias():
        row = lax.broadcasted_iota(jnp.int32, (Q_SUB, K_WIN), 0)
        col = lax.broadcasted_iota(jnp.int32, (Q_SUB, K_WIN), 1)
        for e in range(3):
            dist = jnp.abs(col - row - (2 - e) * HALF_BAND)
            distf = dist.astype(F32)
            for h in range(HEADS_PER_GROUP):
                bias_ref[e, h // 2, (h % 2) * Q_SUB:(h % 2 + 1) * Q_SUB, :] = jnp.where(
                    dist <= HALF_BAND, (-LOG2_E * slope_dist[h]) * distf, MASK_VALUE)

    lane = lax.broadcasted_iota(jnp.int32, (1, LANES), 1)
    low_half = lane < HEAD_DIM

    def sub_block(cls, j):
        blk = pl.program_id(2) * n_sub + j
        m0 = blk * Q_SUB
        start = pl.multiple_of(jnp.clip(m0 - HALF_BAND, 0, length - K_WIN), HALF_BAND)
        edge = jnp.where(blk == 0, 2, jnp.where(blk == n_blocks - 1, 0, 1))
        rows = slice(j * Q_SUB, (j + 1) * Q_SUB)
        lse = jnp.zeros((Q_SUB, LANES), F32)
        for pair in range(HEADS_PER_GROUP // 2):
            cols = slice(pair * LANES, (pair + 1) * LANES)
            qs = q_ref[cls, rows, cols]
            ks = k_ref[cls, pl.ds(start, K_WIN), cols]
            vs = v_ref[cls, pl.ds(start, K_WIN), cols]
            zero = jnp.zeros_like(qs)
            q2 = jnp.concatenate([jnp.where(low_half, qs, zero), jnp.where(low_half, zero, qs)], axis=0)
            s = lax.dot_general(q2, ks, (((1,), (1,)), ((), ())), preferred_element_type=F32)
            s = s + bias_ref[edge, pair]
            m = jnp.max(s, axis=-1, keepdims=True)
            p = jnp.exp2(s - m).astype(BF16)
            out = _dot(p, jnp.concatenate([vs, jnp.ones_like(vs)], axis=1))
            num = jnp.where(low_half, out[:Q_SUB, :LANES], out[Q_SUB:, :LANES])
            den = jnp.where(low_half, out[:Q_SUB, LANES:], out[Q_SUB:, LANES:])
            m2 = jnp.where(low_half, m[:Q_SUB], m[Q_SUB:])
            o_ref[cls, rows, cols] = (num / den).astype(BF16)
            lse = jnp.where((lane == pair) | (lane == HEAD_DIM + pair),
                            (m2 + jnp.log2(den)) * LN_2, lse)
        lse_ref[cls, rows, :] = lse

    for cls in range(n_cls):
        for j in range(n_sub):
            sub_block(cls, j)


def _attention_group(qkv, group, queries_per_step):
    batch, r, length, _ = qkv.shape
    bq = min(queries_per_step, length)
    n_cls = queries_per_step // bq
    slope_dist = tuple(float(_SLOPES[group * HEADS_PER_GROUP + h] * r) for h in range(HEADS_PER_GROUP))
    q_spec = pl.BlockSpec((None, n_cls, bq, SLOT_WIDTH), lambda b, c, i: (b, c, i, 0))
    k_spec = pl.BlockSpec((None, n_cls, length, SLOT_WIDTH), lambda b, c, i: (b, c, 0, 1))
    v_spec = pl.BlockSpec((None, n_cls, length, SLOT_WIDTH), lambda b, c, i: (b, c, 0, 2))
    out_spec = pl.BlockSpec((None, n_cls, bq, SLOT_WIDTH), lambda b, c, i: (b, c, i, 0))
    lse_spec = pl.BlockSpec((None, n_cls, bq, LANES), lambda b, c, i: (b, c, i, 0))
    return pl.pallas_call(
        functools.partial(_attn_kernel, n_cls=n_cls, n_sub=bq // Q_SUB, length=length,
                          slope_dist=slope_dist),
        grid=(batch, r // n_cls, length // bq),
        in_specs=[q_spec, k_spec, v_spec],
        out_specs=[out_spec, lse_spec],
        out_shape=[jax.ShapeDtypeStruct((batch, r, length, SLOT_WIDTH), BF16),
                   jax.ShapeDtypeStruct((batch, r, length, LANES), F32)],
        scratch_shapes=[pltpu.VMEM((3, HEADS_PER_GROUP // 2, 2 * Q_SUB, K_WIN), F32)],
        compiler_params=_params(3),
        name=f"attn_g{group}",
    )(qkv, qkv, qkv)


def _interleave(src_ref, dst_ref, r, tm):
    n = tm // r
    n_slabs = dst_ref.shape[0]
    for c in range(r):
        piece = src_ref[c].astype(F32)
        for s in range(n_slabs):
            dst_ref[s, pl.ds(c, n, stride=r), :] = piece[:, s * LANES:(s + 1) * LANES]
    return jnp.concatenate([dst_ref[s] for s in range(n_slabs)], axis=1)


def _head_expansion():
    row = lax.broadcasted_iota(jnp.int32, (LANES, SLOT_WIDTH), 0)
    col = lax.broadcasted_iota(jnp.int32, (LANES, SLOT_WIDTH), 1)
    shift = HEAD_DIM.bit_length() - 1
    pair, odd = row & (HEAD_DIM - 1), lax.shift_right_logical(row, shift)
    hit = (pair < HEADS_PER_GROUP // 2) & (lax.shift_right_logical(col, shift) == 2 * pair + odd)
    e = jnp.where(hit, 1.0, 0.0).astype(BF16)
    return jnp.concatenate([e, e], axis=0)


def _mix_kernel(cact_ref, wco_ref, o0_ref, o1_ref, o2_ref, l0_ref, l1_ref, l2_ref, wao_ref,
                gate_ref, wout_ref, x_ref, gffn_ref, h_ref, un_ref, il_ref, *, tm):
    conv_out = _dot(cact_ref[...], wco_ref[...])

    r1, r2 = ATTN_GROUPS[1][1], ATTN_GROUPS[2][1]
    l0 = l0_ref[...]
    l1 = _interleave(l1_ref, il_ref.at[0:1], r1, tm)
    l2 = _interleave(l2_ref, il_ref.at[1:2], r2, tm)
    mx = jnp.maximum(jnp.maximum(l0, l1), l2)
    e0, e1, e2 = jnp.exp(l0 - mx), jnp.exp(l1 - mx), jnp.exp(l2 - mx)
    inv = 1.0 / (e0 + e1 + e2)
    expansion = _head_expansion()

    def spread(w):
        hi = w.astype(BF16)
        lo = (w - hi.astype(F32)).astype(BF16)
        return _dot(jnp.concatenate([hi, lo], axis=1), expansion)

    o0 = o0_ref[...].astype(F32)
    o1 = _interleave(o1_ref, il_ref.at[2:2 + SLOT_SLABS], r1, tm)
    o2 = _interleave(o2_ref, il_ref.at[2 + SLOT_SLABS:2 + 2 * SLOT_SLABS], r2, tm)
    attn = spread(e0 * inv) * o0 + spread(e1 * inv) * o1 + spread(e2 * inv) * o2
    attn_out = _dot(attn.astype(BF16), wao_ref[...])

    merged = (gate_ref[:, :D_MODEL].astype(F32) * conv_out
              + gate_ref[:, D_MODEL:].astype(F32) * attn_out)
    h = x_ref[...] + _dot(merged.astype(BF16), wout_ref[...])
    h_ref[...] = h
    un_ref[...] = (h * _rms_scale(h) * gffn_ref[...]).astype(BF16)


def _mix(cact, wco, outs, lses, wao, gates, wout, x2, gffn, seq, tm):
    t = x2.shape[0]
    tiles = seq // tm
    row = lambda width: pl.BlockSpec((tm, width), lambda i: (i, 0))

    def class_major(r, width):
        return pl.BlockSpec((None, r, tm // r, width), lambda i: (i // tiles, 0, i % tiles, 0))

    def attn_specs(width):
        return [row(width)] + [class_major(ATTN_GROUPS[g][1], width) for g in (1, 2)]

    flat0 = lambda a: a.reshape(t, a.shape[-1])
    return pl.pallas_call(
        functools.partial(_mix_kernel, tm=tm),
        grid=(t // tm,),
        in_specs=[row(CONV_CH), _resident(wco.shape), *attn_specs(SLOT_WIDTH), *attn_specs(LANES),
                  _resident(wao.shape), row(2 * D_MODEL), _resident(wout.shape), row(D_MODEL),
                  _resident((1, D_MODEL))],
        out_specs=[row(D_MODEL), row(D_MODEL)],
        out_shape=[jax.ShapeDtypeStruct((t, D_MODEL), F32),
                   jax.ShapeDtypeStruct((t, D_MODEL), BF16)],
        scratch_shapes=[pltpu.VMEM((2 + 2 * SLOT_SLABS, tm, LANES), F32)],
        compiler_params=_params(1),
        name="mix",
    )(cact, wco, flat0(outs[0]), outs[1], outs[2], flat0(lses[0]), lses[1], lses[2], wao, gates,
      wout, x2, gffn)


_FF_CHUNK = 256
_FF_SLABS = _FF_CHUNK // LANES
_FF_SLOTS = 3


def _halo_specs(tm, width, n_rows):
    per = tm // HALO
    last = n_rows // HALO - 1
    prev = pl.BlockSpec((HALO, width), lambda i: (jnp.maximum(i * per - 1, 0), 0))
    cur = pl.BlockSpec((tm, width), lambda i: (i, 0))
    nxt = pl.BlockSpec((HALO, width), lambda i: (jnp.minimum((i + 1) * per, last), 0))
    return prev, cur, nxt


def _ffn_kernel(up_ref_prev, uc_ref, un_ref_next, h_ref, wup_ref, dww_ref, dwb_ref, wdn_ref,
                gfin_ref, y_ref, uext_ref, ab_ref, acc_ref, perm_ref, *, tm, tiles_per_seq):
    pos = pl.program_id(0) % tiles_per_seq
    row_id = lax.broadcasted_iota(jnp.int32, (HALO, 1), 0)
    zero = jnp.zeros_like(up_ref_prev)
    uext_ref[0:HALO, :] = jnp.where(
        (row_id == 0) & (pos != tiles_per_seq - 1), un_ref_next[...],
        jnp.where((row_id == HALO - 1) & (pos != 0), up_ref_prev[...], zero))
    uext_ref[HALO:, :] = uc_ref[...]
    uext = uext_ref[...]
    n_chunks = D_FF // _FF_CHUNK
    half_rows = tm // 2

    def up_proj(c):
        for half in range(2):
            col0 = half * D_FF + c * _FF_CHUNK
            res = _dot(uext, wup_ref[:, col0:col0 + _FF_CHUNK])
            for sl in range(_FF_SLABS):
                piece = res[:, sl * LANES:(sl + 1) * LANES]
                ab_ref[c % _FF_SLOTS, half, sl, 0:HALO + tm, :] = piece
                ab_ref[c % _FF_SLOTS, half, sl, HALO + tm:HALO + tm + 1, :] = piece[0:1]

    def conv3_even_odd(src_ref, col0):
        cols = slice(col0, col0 + LANES)
        w0, w1, w2 = (dww_ref[j:j + 1, cols] for j in range(3))
        bias = dwb_ref[:, cols]
        win = [src_ref[pl.ds(HALO - 1 + k, half_rows, stride=2), :] for k in range(4)]
        even = w0 * win[0] + w1 * win[1] + w2 * win[2] + bias
        odd = w0 * win[1] + w1 * win[2] + w2 * win[3] + bias
        return even, odd

    for c in range(_FF_SLOTS - 1):
        up_proj(c)
    for c in range(n_chunks):
        if c + _FF_SLOTS - 1 < n_chunks:
            up_proj(c + _FF_SLOTS - 1)
        ca = c * _FF_CHUNK
        evens, odds = [], []
        for sl in range(_FF_SLABS):
            a_e, a_o = conv3_even_odd(ab_ref.at[c % _FF_SLOTS, 0, sl], ca + sl * LANES)
            v_e, v_o = conv3_even_odd(ab_ref.at[c % _FF_SLOTS, 1, sl], D_FF + ca + sl * LANES)
            evens.append((_silu(a_e) * v_e).astype(BF16))
            odds.append((_silu(a_o) * v_o).astype(BF16))
        s = jnp.concatenate([jnp.concatenate(evens, axis=1), jnp.concatenate(odds, axis=1)], axis=0)
        part = _dot(s, wdn_ref[ca:ca + _FF_CHUNK, :])
        if c == 0:
            acc_ref[...] = part
        else:
            acc_ref[...] += part

    for sl in range(N_SLABS):
        cols = slice(sl * LANES, (sl + 1) * LANES)
        perm_ref[sl, pl.ds(0, half_rows, stride=2), :] = acc_ref[0:half_rows, cols]
        perm_ref[sl, pl.ds(1, half_rows, stride=2), :] = acc_ref[half_rows:tm, cols]
    h = h_ref[...] + jnp.concatenate([perm_ref[sl] for sl in range(N_SLABS)], axis=1)
    y_ref[...] = h * _rms_scale(h) * gfin_ref[...]


def _ffn(un, h, wup, dww, dwb, wdn, gfin, seq, tm):
    t = h.shape[0]
    row = lambda width: pl.BlockSpec((tm, width), lambda i: (i, 0))
    up, uc, unx = _halo_specs(tm, D_MODEL, t)
    return pl.pallas_call(
        functools.partial(_ffn_kernel, tm=tm, tiles_per_seq=seq // tm),
        grid=(t // tm,),
        in_specs=[up, uc, unx, row(D_MODEL), _resident(wup.shape), _resident(dww.shape),
                  _resident(dwb.shape), _resident(wdn.shape), _resident((1, D_MODEL))],
        out_specs=row(D_MODEL),
        out_shape=jax.ShapeDtypeStruct((t, D_MODEL), F32),
        scratch_shapes=[pltpu.VMEM((HALO + tm, D_MODEL), BF16),
                        pltpu.VMEM((_FF_SLOTS, 2, _FF_SLABS, HALO + tm + SUBLANES, LANES), F32),
                        pltpu.VMEM((tm, D_MODEL), F32),
                        pltpu.VMEM((N_SLABS, tm, LANES), F32)],
        compiler_params=_params(1),
        name="ffn",
    )(un, un, un, h, wup, dww, dwb, wdn, gfin)


def kernel(x, norm_mix_g, w_in, b_gate, conv_dw_w, conv_dw_b, conv_ln_g, conv_ln_b, w_conv_out,
           w_attn_out, w_out, norm_ffn_g, w_up, ffn_dw_w, ffn_dw_b, w_down, norm_final_g):
    batch, seq, d = x.shape
    assert w_in.shape[0] == 1, "single-layer block"
    x2 = x.reshape(batch * seq, d)
    cact, gates, qkv0, qkv1, qkv2 = _in_proj(
        x2, norm_mix_g, w_in[0].astype(BF16), b_gate, conv_dw_w[0], conv_dw_b, conv_ln_g,
        conv_ln_b, batch, seq, tm=256)
    qkv0 = qkv0.reshape(batch, 1, seq, QKV_WIDTH)
    outs, lses = zip(*[_attention_group(qkv, g, queries_per_step=1024)
                       for g, qkv in enumerate((qkv0, qkv1, qkv2))])
    h2, un = _mix(cact, w_conv_out[0].astype(BF16), outs, lses, w_attn_out[0].astype(BF16), gates,
                  w_out[0].astype(BF16), x2, norm_ffn_g, seq, tm=512)
    y2 = _ffn(un, h2, w_up[0].astype(BF16), ffn_dw_w[0], ffn_dw_b, w_down[0].astype(BF16),
              norm_final_g[None], seq, tm=512)
    return y2.reshape(batch, seq, d)
```

```python
import functools

import jax
import jax.numpy as jnp
import numpy as np
from jax import lax
from jax.experimental import pallas as pl
from jax.experimental.pallas import tpu as pltpu

F32 = jnp.float32
BF16 = jnp.bfloat16

D_MODEL = 1024
CONV_CH = 1024
CONV_WIDTH = 31
ATTN_GROUPS = ((128, 1), (512, 4), (2048, 16))
N_GROUPS = len(ATTN_GROUPS)
HEADS_PER_GROUP = 8
HEAD_DIM = 64
N_ATTN_HEADS = HEADS_PER_GROUP * N_GROUPS
ATTN_WIDTH = N_ATTN_HEADS * HEAD_DIM
SLOT_WIDTH = HEADS_PER_GROUP * HEAD_DIM
QKV_WIDTH = 3 * SLOT_WIDTH
D_FF = 2816
RMS_EPS = 1e-6
LN_EPS = 1e-5
MASK_VALUE = -1e30
LOG2_E = float(np.log2(np.e))
LN_2 = float(np.log(2.0))

LANES = 128
SUBLANES = 8
N_SLABS = D_MODEL // LANES
SLOT_SLABS = SLOT_WIDTH // LANES
HALF_BAND = 64
Q_SUB = 128
K_WIN = Q_SUB + 2 * HALF_BAND
HALO = 16
CONV_HALF = CONV_WIDTH // 2
V7X_VMEM_BYTES = 64 * 1024 * 1024
VMEM_LIMIT = V7X_VMEM_BYTES * 7 // 8

_OFF_A, _OFF_G, _OFF_Q = 0, CONV_CH, 2 * CONV_CH
_OFF_K, _OFF_V = _OFF_Q + ATTN_WIDTH, _OFF_Q + 2 * ATTN_WIDTH
_OFF_GATE = _OFF_Q + 3 * ATTN_WIDTH
_OFF_END = _OFF_GATE + 2 * D_MODEL

_SLOPES = (np.float32(2.0) ** (np.float32(-8.0) * np.arange(1, N_ATTN_HEADS + 1, dtype=np.float32)
                               / np.float32(N_ATTN_HEADS))).astype(np.float64)


def _sigmoid(x):
    return 0.5 * jnp.tanh(0.5 * x) + 0.5


def _silu(x):
    return x * _sigmoid(x)


def _rms_scale(xf):
    return lax.rsqrt(jnp.mean(xf * xf, axis=-1, keepdims=True) + RMS_EPS)


def _dot(a, b):
    return jnp.dot(a, b, preferred_element_type=F32)


def _resident(shape):
    return pl.BlockSpec(shape, lambda *_: (0,) * len(shape), pipeline_mode=pl.Buffered(1))


def _params(n_axes):
    return pltpu.CompilerParams(dimension_semantics=("arbitrary",) * n_axes,
                                vmem_limit_bytes=VMEM_LIMIT)


_CONV_ROWS = 32
_GLU_UNIT = 256
_PROJ_UNIT = 256


def _in_proj_kernel(x_ref, g_ref, w_ref, bg_ref, dww_ref, dwb_ref, lng_ref, lnb_ref,
                    cact_ref, gate_ref, qkv0_ref, qkv1_ref, qkv2_ref,
                    us_ref, zs_ref, ys_ref, zc_ref, *, tm, tiles_per_seq):
    step = pl.program_id(0)
    starts_seq = step % tiles_per_seq == 0

    @pl.when(step == 0)
    def _init():
        zs_ref[...] = jnp.zeros_like(zs_ref)

    xf = x_ref[...]
    u = xf * _rms_scale(xf) * g_ref[...]
    ub = u.astype(BF16)
    for s in range(N_SLABS):
        us_ref[s] = u[:, s * LANES:(s + 1) * LANES]

    def proj(lhs, lo, hi):
        return _dot(lhs, w_ref[:, lo:hi])

    def glu_unit(k):
        lo = k * _GLU_UNIT
        zk = proj(ub, _OFF_A + lo, _OFF_A + lo + _GLU_UNIT) * _sigmoid(
            proj(ub, _OFF_G + lo, _OFF_G + lo + _GLU_UNIT))
        for j in range(_GLU_UNIT // LANES):
            s = lo // LANES + j
            piece = zk[:, j * LANES:(j + 1) * LANES]
            zc_ref[s] = piece
            zs_ref[s, HALO + tm:, :] = jnp.where(starts_seq, 0.0, piece[0:HALO])

    first = HALO - CONV_HALF
    half_rows = _CONV_ROWS // 2

    def conv_unit(r0, s):
        cols = slice(s * LANES, (s + 1) * LANES)
        tap = lambda j: dww_ref[j:j + 1, cols]
        acc_e = acc_o = jnp.broadcast_to(dwb_ref[:, cols], (half_rows, LANES))
        for k in range(CONV_WIDTH + 1):
            win = zs_ref[s, pl.ds(r0 + first + k, half_rows, stride=2), :]
            if k < CONV_WIDTH:
                acc_e = acc_e + tap(k) * win
            if k > 0:
                acc_o = acc_o + tap(k - 1) * win
        ys_ref[s, pl.ds(r0, half_rows, stride=2), :] = acc_e
        ys_ref[s, pl.ds(r0 + 1, half_rows, stride=2), :] = acc_o

    def norm_unit(r0):
        rows = slice(r0, r0 + _CONV_ROWS)
        y = jnp.concatenate([ys_ref[s, rows, :] for s in range(N_SLABS)], axis=1)
        mu = jnp.mean(y, axis=-1, keepdims=True)
        yc = y - mu
        var = jnp.mean(yc * yc, axis=-1, keepdims=True)
        yn = yc * lax.rsqrt(var + LN_EPS) * lng_ref[...] + lnb_ref[...]
        cact_ref[rows, :] = _silu(yn).astype(BF16)

    def gate_unit(k):
        cols = slice(k * _PROJ_UNIT, (k + 1) * _PROJ_UNIT)
        logits = proj(ub, _OFF_GATE + cols.start, _OFF_GATE + cols.stop) + bg_ref[:, cols]
        gate_ref[:, cols] = _sigmoid(logits).astype(BF16)

    def class_major_lhs(r):
        n = tm // r
        return jnp.concatenate(
            [jnp.concatenate([us_ref[s, pl.ds(c, n, stride=r), :] for s in range(N_SLABS)], axis=1)
             for c in range(r)], axis=0).astype(BF16)

    def qkv_unit(out_ref, lhs, group, which, part):
        off = (_OFF_Q, _OFF_K, _OFF_V)[which] + group * SLOT_WIDTH + part * _PROJ_UNIT
        res = proj(lhs, off, off + _PROJ_UNIT)
        if which == 0:
            res = res * (LOG2_E * HEAD_DIM ** -0.5)
        res = res.astype(BF16)
        cols = slice(which * SLOT_WIDTH + part * _PROJ_UNIT,
                     which * SLOT_WIDTH + (part + 1) * _PROJ_UNIT)
        if group == 0:
            out_ref[:, cols] = res
        else:
            r = ATTN_GROUPS[group][1]
            for c in range(r):
                out_ref[c, :, cols] = res[c * (tm // r):(c + 1) * (tm // r)]

    @pl.when(step >= 0)
    def _interleaved():
        lhs = {0: ub, 1: class_major_lhs(ATTN_GROUPS[1][1]), 2: class_major_lhs(ATTN_GROUPS[2][1])}
        matmul_units = [functools.partial(glu_unit, k) for k in range(CONV_CH // _GLU_UNIT)]
        matmul_units += [functools.partial(gate_unit, k) for k in range(2 * D_MODEL // _PROJ_UNIT)]
        matmul_units += [functools.partial(qkv_unit, out_ref, lhs[group], group, which, part)
                         for group, out_ref in enumerate((qkv0_ref, qkv1_ref, qkv2_ref))
                         for which in range(3) for part in range(SLOT_WIDTH // _PROJ_UNIT)]
        vector_units = []
        for r0 in range(0, tm, _CONV_ROWS):
            vector_units += [functools.partial(conv_unit, r0, s) for s in range(N_SLABS)]
        done = 0
        for i, unit in enumerate(matmul_units):
            unit()
            due = -(-(i + 1) * len(vector_units) // len(matmul_units))
            for vec in vector_units[done:due]:
                vec()
            done = due
        assert done == len(vector_units)

    for r0 in range(0, tm, _CONV_ROWS):
        norm_unit(r0)

    for s in range(N_SLABS):
        tail = zs_ref[s, tm:tm + HALO, :]
        zs_ref[s, 0:HALO, :] = jnp.where(starts_seq, 0.0, tail)
        zs_ref[s, HALO:HALO + tm, :] = zc_ref[s]


def _in_proj(x2, g, w_in, b_gate, dww, dwb, lng, lnb, batch, seq, tm):
    t = x2.shape[0]
    tiles = seq // tm
    n_tiles = t // tm
    cur = lambda i: jnp.minimum(i, n_tiles - 1)
    row = lambda width: pl.BlockSpec((tm, width), lambda i: (cur(i), 0))
    lagged = pl.BlockSpec((tm, CONV_CH), lambda i: (jnp.maximum(i - 1, 0), 0))

    def class_major(r):
        return pl.BlockSpec((None, r, tm // r, QKV_WIDTH),
                            lambda i: (cur(i) // tiles, 0, cur(i) % tiles, 0))

    r1, r2 = ATTN_GROUPS[1][1], ATTN_GROUPS[2][1]
    vec = _resident((1, CONV_CH))
    return pl.pallas_call(
        functools.partial(_in_proj_kernel, tm=tm, tiles_per_seq=tiles),
        grid=(n_tiles + 1,),
        in_specs=[row(D_MODEL), _resident((1, D_MODEL)), _resident(w_in.shape),
                  _resident((1, 2 * D_MODEL)), _resident(dww.shape), vec, vec, vec],
        out_specs=[lagged, row(2 * D_MODEL), row(QKV_WIDTH), class_major(r1), class_major(r2)],
        out_shape=[jax.ShapeDtypeStruct((t, CONV_CH), BF16),
                   jax.ShapeDtypeStruct((t, 2 * D_MODEL), BF16),
                   jax.ShapeDtypeStruct((t, QKV_WIDTH), BF16),
                   jax.ShapeDtypeStruct((batch, r1, seq // r1, QKV_WIDTH), BF16),
                   jax.ShapeDtypeStruct((batch, r2, seq // r2, QKV_WIDTH), BF16)],
        scratch_shapes=[pltpu.VMEM((N_SLABS, tm, LANES), F32),
                        pltpu.VMEM((N_SLABS, tm + 2 * HALO, LANES), F32),
                        pltpu.VMEM((N_SLABS, tm, LANES), F32),
                        pltpu.VMEM((N_SLABS, tm, LANES), F32)],
        compiler_params=_params(1),
        name="in_proj",
    )(x2, g, w_in, b_gate, dww, dwb, lng, lnb)


def _attn_kernel(q_ref, k_ref, v_ref, o_ref, lse_ref, bias_ref, *, n_cls, n_sub, length,
                 slope_dist):
    n_blocks = length // Q_SUB

    @pl.when((pl.program_id(0) == 0) & (pl.program_id(1) == 0) & (pl.program_id(2) == 0))
    def _fill_bias():
        row = lax.broadcasted_iota(jnp.int32, (Q_SUB, K_WIN), 0)
        col = lax.broadcasted_iota(jnp.int32, (Q_SUB, K_WIN), 1)
        for e in range(3):
            dist = jnp.abs(col - row - (2 - e) * HALF_BAND)
            distf = dist.astype(F32)
            for h in range(HEADS_PER_GROUP):
                bias_ref[e, h // 2, (h % 2) * Q_SUB:(h % 2 + 1) * Q_SUB, :] = jnp.where(
                    dist <= HALF_BAND, (-LOG2_E * slope_dist[h]) * distf, MASK_VALUE)

    lane = lax.broadcasted_iota(jnp.int32, (1, LANES), 1)
    low_half = lane < HEAD_DIM

    def sub_block(cls, j):
        blk = pl.program_id(2) * n_sub + j
        m0 = blk * Q_SUB
        start = pl.multiple_of(jnp.clip(m0 - HALF_BAND, 0, length - K_WIN), HALF_BAND)
        edge = jnp.where(blk == 0, 2, jnp.where(blk == n_blocks - 1, 0, 1))
        rows = slice(j * Q_SUB, (j + 1) * Q_SUB)
        lse = jnp.zeros((Q_SUB, LANES), F32)
        for pair in range(HEADS_PER_GROUP // 2):
            cols = slice(pair * LANES, (pair + 1) * LANES)
            qs = q_ref[cls, rows, cols]
            ks = k_ref[cls, pl.ds(start, K_WIN), cols]
            vs = v_ref[cls, pl.ds(start, K_WIN), cols]
            zero = jnp.zeros_like(qs)
            q2 = jnp.concatenate([jnp.where(low_half, qs, zero), jnp.where(low_half, zero, qs)], axis=0)
            s = lax.dot_general(q2, ks, (((1,), (1,)), ((), ())), preferred_element_type=F32)
            s = s + bias_ref[edge, pair]
            m = jnp.max(s, axis=-1, keepdims=True)
            p = jnp.exp2(s - m).astype(BF16)
            out = _dot(p, jnp.concatenate([vs, jnp.ones_like(vs)], axis=1))
            num = jnp.where(low_half, out[:Q_SUB, :LANES], out[Q_SUB:, :LANES])
            den = jnp.where(low_half, out[:Q_SUB, LANES:], out[Q_SUB:, LANES:])
            m2 = jnp.where(low_half, m[:Q_SUB], m[Q_SUB:])
            o_ref[cls, rows, cols] = (num / den).astype(BF16)
            lse = jnp.where((lane == pair) | (lane == HEAD_DIM + pair),
                            (m2 + jnp.log2(den)) * LN_2, lse)
        lse_ref[cls, rows, :] = lse

    for cls in range(n_cls):
        for j in range(n_sub):
            sub_block(cls, j)


def _attention_group(qkv, group, queries_per_step):
    batch, r, length, _ = qkv.shape
    bq = min(queries_per_step, length)
    n_cls = queries_per_step // bq
    slope_dist = tuple(float(_SLOPES[group * HEADS_PER_GROUP + h] * r) for h in range(HEADS_PER_GROUP))
    q_spec = pl.BlockSpec((None, n_cls, bq, SLOT_WIDTH), lambda b, c, i: (b, c, i, 0))
    k_spec = pl.BlockSpec((None, n_cls, length, SLOT_WIDTH), lambda b, c, i: (b, c, 0, 1))
    v_spec = pl.BlockSpec((None, n_cls, length, SLOT_WIDTH), lambda b, c, i: (b, c, 0, 2))
    out_spec = pl.BlockSpec((None, n_cls, bq, SLOT_WIDTH), lambda b, c, i: (b, c, i, 0))
    lse_spec = pl.BlockSpec((None, n_cls, bq, LANES), lambda b, c, i: (b, c, i, 0))
    return pl.pallas_call(
        functools.partial(_attn_kernel, n_cls=n_cls, n_sub=bq // Q_SUB, length=length,
                          slope_dist=slope_dist),
        grid=(batch, r // n_cls, length // bq),
        in_specs=[q_spec, k_spec, v_spec],
        out_specs=[out_spec, lse_spec],
        out_shape=[jax.ShapeDtypeStruct((batch, r, length, SLOT_WIDTH), BF16),
                   jax.ShapeDtypeStruct((batch, r, length, LANES), F32)],
        scratch_shapes=[pltpu.VMEM((3, HEADS_PER_GROUP // 2, 2 * Q_SUB, K_WIN), F32)],
        compiler_params=_params(3),
        name=f"attn_g{group}",
    )(qkv, qkv, qkv)


def _interleave(src_ref, dst_ref, r, tm):
    n = tm // r
    n_slabs = dst_ref.shape[0]
    for c in range(r):
        piece = src_ref[c].astype(F32)
        for s in range(n_slabs):
            dst_ref[s, pl.ds(c, n, stride=r), :] = piece[:, s * LANES:(s + 1) * LANES]
    return jnp.concatenate([dst_ref[s] for s in range(n_slabs)], axis=1)


def _head_expansion():
    row = lax.broadcasted_iota(jnp.int32, (LANES, SLOT_WIDTH), 0)
    col = lax.broadcasted_iota(jnp.int32, (LANES, SLOT_WIDTH), 1)
    shift = HEAD_DIM.bit_length() - 1
    pair, odd = row & (HEAD_DIM - 1), lax.shift_right_logical(row, shift)
    hit = (pair < HEADS_PER_GROUP // 2) & (lax.shift_right_logical(col, shift) == 2 * pair + odd)
    e = jnp.where(hit, 1.0, 0.0).astype(BF16)
    return jnp.concatenate([e, e], axis=0)


def _mix_kernel(cact_ref, wco_ref, o0_ref, o1_ref, o2_ref, l0_ref, l1_ref, l2_ref, wao_ref,
                gate_ref, wout_ref, x_ref, gffn_ref, h_ref, un_ref, il_ref, *, tm):
    conv_out = _dot(cact_ref[...], wco_ref[...])

    r1, r2 = ATTN_GROUPS[1][1], ATTN_GROUPS[2][1]
    l0 = l0_ref[...]
    l1 = _interleave(l1_ref, il_ref.at[0:1], r1, tm)
    l2 = _interleave(l2_ref, il_ref.at[1:2], r2, tm)
    mx = jnp.maximum(jnp.maximum(l0, l1), l2)
    e0, e1, e2 = jnp.exp(l0 - mx), jnp.exp(l1 - mx), jnp.exp(l2 - mx)
    inv = 1.0 / (e0 + e1 + e2)
    expansion = _head_expansion()

    def spread(w):
        hi = w.astype(BF16)
        lo = (w - hi.astype(F32)).astype(BF16)
        return _dot(jnp.concatenate([hi, lo], axis=1), expansion)

    o0 = o0_ref[...].astype(F32)
    o1 = _interleave(o1_ref, il_ref.at[2:2 + SLOT_SLABS], r1, tm)
    o2 = _interleave(o2_ref, il_ref.at[2 + SLOT_SLABS:2 + 2 * SLOT_SLABS], r2, tm)
    attn = spread(e0 * inv) * o0 + spread(e1 * inv) * o1 + spread(e2 * inv) * o2
    attn_out = _dot(attn.astype(BF16), wao_ref[...])

    merged = (gate_ref[:, :D_MODEL].astype(F32) * conv_out
              + gate_ref[:, D_MODEL:].astype(F32) * attn_out)
    h = x_ref[...] + _dot(merged.astype(BF16), wout_ref[...])
    h_ref[...] = h
    un_ref[...] = (h * _rms_scale(h) * gffn_ref[...]).astype(BF16)


def _mix(cact, wco, outs, lses, wao, gates, wout, x2, gffn, seq, tm):
    t = x2.shape[0]
    tiles = seq // tm
    row = lambda width: pl.BlockSpec((tm, width), lambda i: (i, 0))

    def class_major(r, width):
        return pl.BlockSpec((None, r, tm // r, width), lambda i: (i // tiles, 0, i % tiles, 0))

    def attn_specs(width):
        return [row(width)] + [class_major(ATTN_GROUPS[g][1], width) for g in (1, 2)]

    flat0 = lambda a: a.reshape(t, a.shape[-1])
    return pl.pallas_call(
        functools.partial(_mix_kernel, tm=tm),
        grid=(t // tm,),
        in_specs=[row(CONV_CH), _resident(wco.shape), *attn_specs(SLOT_WIDTH), *attn_specs(LANES),
                  _resident(wao.shape), row(2 * D_MODEL), _resident(wout.shape), row(D_MODEL),
                  _resident((1, D_MODEL))],
        out_specs=[row(D_MODEL), row(D_MODEL)],
        out_shape=[jax.ShapeDtypeStruct((t, D_MODEL), F32),
                   jax.ShapeDtypeStruct((t, D_MODEL), BF16)],
        scratch_shapes=[pltpu.VMEM((2 + 2 * SLOT_SLABS, tm, LANES), F32)],
        compiler_params=_params(1),
        name="mix",
    )(cact, wco, flat0(outs[0]), outs[1], outs[2], flat0(lses[0]), lses[1], lses[2], wao, gates,
      wout, x2, gffn)


_FF_CHUNK = 256
_FF_SLABS = _FF_CHUNK // LANES
_FF_SLOTS = 3


def _halo_specs(tm, width, n_rows):
    per = tm // HALO
    last = n_rows // HALO - 1
    prev = pl.BlockSpec((HALO, width), lambda i: (jnp.maximum(i * per - 1, 0), 0))
    cur = pl.BlockSpec((tm, width), lambda i: (i, 0))
    nxt = pl.BlockSpec((HALO, width), lambda i: (jnp.minimum((i + 1) * per, last), 0))
    return prev, cur, nxt


def _ffn_kernel(up_ref_prev, uc_ref, un_ref_next, h_ref, wup_ref, dww_ref, dwb_ref, wdn_ref,
                gfin_ref, y_ref, uext_ref, ab_ref, acc_ref, perm_ref, *, tm, tiles_per_seq):
    pos = pl.program_id(0) % tiles_per_seq
    row_id = lax.broadcasted_iota(jnp.int32, (HALO, 1), 0)
    zero = jnp.zeros_like(up_ref_prev)
    uext_ref[0:HALO, :] = jnp.where(
        (row_id == 0) & (pos != tiles_per_seq - 1), un_ref_next[...],
        jnp.where((row_id == HALO - 1) & (pos != 0), up_ref_prev[...], zero))
    uext_ref[HALO:, :] = uc_ref[...]
    uext = uext_ref[...]
    n_chunks = D_FF // _FF_CHUNK
    half_rows = tm // 2

    def up_proj(c):
        for half in range(2):
            col0 = half * D_FF + c * _FF_CHUNK
            res = _dot(uext, wup_ref[:, col0:col0 + _FF_CHUNK])
            for sl in range(_FF_SLABS):
                piece = res[:, sl * LANES:(sl + 1) * LANES]
                ab_ref[c % _FF_SLOTS, half, sl, 0:HALO + tm, :] = piece
                ab_ref[c % _FF_SLOTS, half, sl, HALO + tm:HALO + tm + 1, :] = piece[0:1]

    def conv3_even_odd(src_ref, col0):
        cols = slice(col0, col0 + LANES)
        w0, w1, w2 = (dww_ref[j:j + 1, cols] for j in range(3))
        bias = dwb_ref[:, cols]
        win = [src_ref[pl.ds(HALO - 1 + k, half_rows, stride=2), :] for k in range(4)]
        even = w0 * win[0] + w1 * win[1] + w2 * win[2] + bias
        odd = w0 * win[1] + w1 * win[2] + w2 * win[3] + bias
        return even, odd

    for c in range(_FF_SLOTS - 1):
        up_proj(c)
    for c in range(n_chunks):
        if c + _FF_SLOTS - 1 < n_chunks:
            up_proj(c + _FF_SLOTS - 1)
        ca = c * _FF_CHUNK
        evens, odds = [], []
        for sl in range(_FF_SLABS):
            a_e, a_o = conv3_even_odd(ab_ref.at[c % _FF_SLOTS, 0, sl], ca + sl * LANES)
            v_e, v_o = conv3_even_odd(ab_ref.at[c % _FF_SLOTS, 1, sl], D_FF + ca + sl * LANES)
            evens.append((_silu(a_e) * v_e).astype(BF16))
            odds.append((_silu(a_o) * v_o).astype(BF16))
        s = jnp.concatenate([jnp.concatenate(evens, axis=1), jnp.concatenate(odds, axis=1)], axis=0)
        part = _dot(s, wdn_ref[ca:ca + _FF_CHUNK, :])
        if c == 0:
            acc_ref[...] = part
        else:
            acc_ref[...] += part

    for sl in range(N_SLABS):
        cols = slice(sl * LANES, (sl + 1) * LANES)
        perm_ref[sl, pl.ds(0, half_rows, stride=2), :] = acc_ref[0:half_rows, cols]
        perm_ref[sl, pl.ds(1, half_rows, stride=2), :] = acc_ref[half_rows:tm, cols]
    h = h_ref[...] + jnp.concatenate([perm_ref[sl] for sl in range(N_SLABS)], axis=1)
    y_ref[...] = h * _rms_scale(h) * gfin_ref[...]


def _ffn(un, h, wup, dww, dwb, wdn, gfin, seq, tm):
    t = h.shape[0]
    row = lambda width: pl.BlockSpec((tm, width), lambda i: (i, 0))
    up, uc, unx = _halo_specs(tm, D_MODEL, t)
    return pl.pallas_call(
        functools.partial(_ffn_kernel, tm=tm, tiles_per_seq=seq // tm),
        grid=(t // tm,),
        in_specs=[up, uc, unx, row(D_MODEL), _resident(wup.shape), _resident(dww.shape),
                  _resident(dwb.shape), _resident(wdn.shape), _resident((1, D_MODEL))],
        out_specs=row(D_MODEL),
        out_shape=jax.ShapeDtypeStruct((t, D_MODEL), F32),
        scratch_shapes=[pltpu.VMEM((HALO + tm, D_MODEL), BF16),
                        pltpu.VMEM((_FF_SLOTS, 2, _FF_SLABS, HALO + tm + SUBLANES, LANES), F32),
                        pltpu.VMEM((tm, D_MODEL), F32),
                        pltpu.VMEM((N_SLABS, tm, LANES), F32)],
        compiler_params=_params(1),
        name="ffn",
    )(un, un, un, h, wup, dww, dwb, wdn, gfin)


def kernel(x, norm_mix_g, w_in, b_gate, conv_dw_w, conv_dw_b, conv_ln_g, conv_ln_b, w_conv_out,
           w_attn_out, w_out, norm_ffn_g, w_up, ffn_dw_w, ffn_dw_b, w_down, norm_final_g):
    batch, seq, d = x.shape
    assert w_in.shape[0] == 1, "single-layer block"
    x2 = x.reshape(batch * seq, d)
    cact, gates, qkv0, qkv1, qkv2 = _in_proj(
        x2, norm_mix_g, w_in[0].astype(BF16), b_gate, conv_dw_w[0], conv_dw_b, conv_ln_g,
        conv_ln_b, batch, seq, tm=256)
    qkv0 = qkv0.reshape(batch, 1, seq, QKV_WIDTH)
    outs, lses = zip(*[_attention_group(qkv, g, queries_per_step=1024)
                       for g, qkv in enumerate((qkv0, qkv1, qkv2))])
    h2, un = _mix(cact, w_conv_out[0].astype(BF16), outs, lses, w_attn_out[0].astype(BF16), gates,
                  w_out[0].astype(BF16), x2, norm_ffn_g, seq, tm=512)
    y2 = _ffn(un, h2, w_up[0].astype(BF16), ffn_dw_w[0], ffn_dw_b, w_down[0].astype(BF16),
              norm_final_g[None], seq, tm=512)
    return y2.reshape(batch, seq, d)
```

```python
import functools

import jax
import jax.numpy as jnp
import numpy as np
from jax import lax
from jax.experimental import pallas as pl
from jax.experimental.pallas import tpu as pltpu

F32 = jnp.float32
BF16 = jnp.bfloat16

D_MODEL = 1024
CONV_CH = 1024
CONV_WIDTH = 31
ATTN_GROUPS = ((128, 1), (512, 4), (2048, 16))
N_GROUPS = len(ATTN_GROUPS)
HEADS_PER_GROUP = 8
HEAD_DIM = 64
N_ATTN_HEADS = HEADS_PER_GROUP * N_GROUPS
ATTN_WIDTH = N_ATTN_HEADS * HEAD_DIM
SLOT_WIDTH = HEADS_PER_GROUP * HEAD_DIM
QKV_WIDTH = 3 * SLOT_WIDTH
D_FF = 2816
RMS_EPS = 1e-6
LN_EPS = 1e-5
MASK_VALUE = -1e30
LOG2_E = float(np.log2(np.e))
LN_2 = float(np.log(2.0))

LANES = 128
SUBLANES = 8
N_SLABS = D_MODEL // LANES
SLOT_SLABS = SLOT_WIDTH // LANES
HALF_BAND = 64
Q_SUB = 128
K_WIN = Q_SUB + 2 * HALF_BAND
HALO = 16
CONV_HALF = CONV_WIDTH // 2
V7X_VMEM_BYTES = 64 * 1024 * 1024
VMEM_LIMIT = V7X_VMEM_BYTES * 7 // 8

_OFF_A, _OFF_G, _OFF_Q = 0, CONV_CH, 2 * CONV_CH
_OFF_K, _OFF_V = _OFF_Q + ATTN_WIDTH, _OFF_Q + 2 * ATTN_WIDTH
_OFF_GATE = _OFF_Q + 3 * ATTN_WIDTH
_OFF_END = _OFF_GATE + 2 * D_MODEL

_SLOPES = (np.float32(2.0) ** (np.float32(-8.0) * np.arange(1, N_ATTN_HEADS + 1, dtype=np.float32)
                               / np.float32(N_ATTN_HEADS))).astype(np.float64)


def _sigmoid(x):
    return 0.5 * jnp.tanh(0.5 * x) + 0.5


def _silu(x):
    return x * _sigmoid(x)


def _rms_scale(xf):
    return lax.rsqrt(jnp.mean(xf * xf, axis=-1, keepdims=True) + RMS_EPS)


def _dot(a, b):
    return jnp.dot(a, b, preferred_element_type=F32)


def _resident(shape):
    return pl.BlockSpec(shape, lambda *_: (0,) * len(shape), pipeline_mode=pl.Buffered(1))


def _params(n_axes):
    return pltpu.CompilerParams(dimension_semantics=("arbitrary",) * n_axes,
                                vmem_limit_bytes=VMEM_LIMIT)


_CONV_ROWS = 32
_GLU_UNIT = 256
_PROJ_UNIT = 256


def _in_proj_kernel(x_ref, g_ref, w_ref, bg_ref, dww_ref, dwb_ref, lng_ref, lnb_ref,
                    cact_ref, gate_ref, qkv0_ref, qkv1_ref, qkv2_ref,
                    us_ref, zs_ref, ys_ref, zc_ref, *, tm, tiles_per_seq):
    step = pl.program_id(0)
    starts_seq = step % tiles_per_seq == 0

    @pl.when(step == 0)
    def _init():
        zs_ref[...] = jnp.zeros_like(zs_ref)

    xf = x_ref[...]
    u = xf * _rms_scale(xf) * g_ref[...]
    ub = u.astype(BF16)
    for s in range(N_SLABS):
        us_ref[s] = u[:, s * LANES:(s + 1) * LANES]

    def proj(lhs, lo, hi):
        return _dot(lhs, w_ref[:, lo:hi])

    def glu_unit(k):
        lo = k * _GLU_UNIT
        zk = proj(ub, _OFF_A + lo, _OFF_A + lo + _GLU_UNIT) * _sigmoid(
            proj(ub, _OFF_G + lo, _OFF_G + lo + _GLU_UNIT))
        for j in range(_GLU_UNIT // LANES):
            s = lo // LANES + j
            piece = zk[:, j * LANES:(j + 1) * LANES]
            zc_ref[s] = piece
            zs_ref[s, HALO + tm:, :] = jnp.where(starts_seq, 0.0, piece[0:HALO])

    first = HALO - CONV_HALF
    half_rows = _CONV_ROWS // 2

    def conv_unit(r0, s):
        cols = slice(s * LANES, (s + 1) * LANES)
        tap = lambda j: dww_ref[j:j + 1, cols]
        acc_e = acc_o = jnp.broadcast_to(dwb_ref[:, cols], (half_rows, LANES))
        for k in range(CONV_WIDTH + 1):
            win = zs_ref[s, pl.ds(r0 + first + k, half_rows, stride=2), :]
            if k < CONV_WIDTH:
                acc_e = acc_e + tap(k) * win
            if k > 0:
                acc_o = acc_o + tap(k - 1) * win
        ys_ref[s, pl.ds(r0, half_rows, stride=2), :] = acc_e
        ys_ref[s, pl.ds(r0 + 1, half_rows, stride=2), :] = acc_o

    def norm_unit(r0):
        rows = slice(r0, r0 + _CONV_ROWS)
        y = jnp.concatenate([ys_ref[s, rows, :] for s in range(N_SLABS)], axis=1)
        mu = jnp.mean(y, axis=-1, keepdims=True)
        yc = y - mu
        var = jnp.mean(yc * yc, axis=-1, keepdims=True)
        yn = yc * lax.rsqrt(var + LN_EPS) * lng_ref[...] + lnb_ref[...]
        cact_ref[rows, :] = _silu(yn).astype(BF16)

    def gate_unit(k):
        cols = slice(k * _PROJ_UNIT, (k + 1) * _PROJ_UNIT)
        logits = proj(ub, _OFF_GATE + cols.start, _OFF_GATE + cols.stop) + bg_ref[:, cols]
        gate_ref[:, cols] = _sigmoid(logits).astype(BF16)

    def class_major_lhs(r):
        n = tm // r
        return jnp.concatenate(
            [jnp.concatenate([us_ref[s, pl.ds(c, n, stride=r), :] for s in range(N_SLABS)], axis=1)
             for c in range(r)], axis=0).astype(BF16)

    def qkv_unit(out_ref, lhs, group, which, part):
        off = (_OFF_Q, _OFF_K, _OFF_V)[which] + group * SLOT_WIDTH + part * _PROJ_UNIT
        res = proj(lhs, off, off + _PROJ_UNIT)
        if which == 0:
            res = res * (LOG2_E * HEAD_DIM ** -0.5)
        res = res.astype(BF16)
        cols = slice(which * SLOT_WIDTH + part * _PROJ_UNIT,
                     which * SLOT_WIDTH + (part + 1) * _PROJ_UNIT)
        if group == 0:
            out_ref[:, cols] = res
        else:
            r = ATTN_GROUPS[group][1]
            for c in range(r):
                out_ref[c, :, cols] = res[c * (tm // r):(c + 1) * (tm // r)]

    @pl.when(step >= 0)
    def _interleaved():
        lhs = {0: ub, 1: class_major_lhs(ATTN_GROUPS[1][1]), 2: class_major_lhs(ATTN_GROUPS[2][1])}
        matmul_units = [functools.partial(glu_unit, k) for k in range(CONV_CH // _GLU_UNIT)]
        matmul_units += [functools.partial(gate_unit, k) for k in range(2 * D_MODEL // _PROJ_UNIT)]
        matmul_units += [functools.partial(qkv_unit, out_ref, lhs[group], group, which, part)
                         for group, out_ref in enumerate((qkv0_ref, qkv1_ref, qkv2_ref))
                         for which in range(3) for part in range(SLOT_WIDTH // _PROJ_UNIT)]
        vector_units = []
        for r0 in range(0, tm, _CONV_ROWS):
            vector_units += [functools.partial(conv_unit, r0, s) for s in range(N_SLABS)]
        done = 0
        for i, unit in enumerate(matmul_units):
            unit()
            due = -(-(i + 1) * len(vector_units) // len(matmul_units))
            for vec in vector_units[done:due]:
                vec()
            done = due
        assert done == len(vector_units)

    for r0 in range(0, tm, _CONV_ROWS):
        norm_unit(r0)

    for s in range(N_SLABS):
        tail = zs_ref[s, tm:tm + HALO, :]
        zs_ref[s, 0:HALO, :] = jnp.where(starts_seq, 0.0, tail)
        zs_ref[s, HALO:HALO + tm, :] = zc_ref[s]


def _in_proj(x2, g, w_in, b_gate, dww, dwb, lng, lnb, batch, seq, tm):
    t = x2.shape[0]
    tiles = seq // tm
    n_tiles = t // tm
    cur = lambda i: jnp.minimum(i, n_tiles - 1)
    row = lambda width: pl.BlockSpec((tm, width), lambda i: (cur(i), 0))
    lagged = pl.BlockSpec((tm, CONV_CH), lambda i: (jnp.maximum(i - 1, 0), 0))

    def class_major(r):
        return pl.BlockSpec((None, r, tm // r, QKV_WIDTH),
                            lambda i: (cur(i) // tiles, 0, cur(i) % tiles, 0))

    r1, r2 = ATTN_GROUPS[1][1], ATTN_GROUPS[2][1]
    vec = _resident((1, CONV_CH))
    return pl.pallas_call(
        functools.partial(_in_proj_kernel, tm=tm, tiles_per_seq=tiles),
        grid=(n_tiles + 1,),
        in_specs=[row(D_MODEL), _resident((1, D_MODEL)), _resident(w_in.shape),
                  _resident((1, 2 * D_MODEL)), _resident(dww.shape), vec, vec, vec],
        out_specs=[lagged, row(2 * D_MODEL), row(QKV_WIDTH), class_major(r1), class_major(r2)],
        out_shape=[jax.ShapeDtypeStruct((t, CONV_CH), BF16),
                   jax.ShapeDtypeStruct((t, 2 * D_MODEL), BF16),
                   jax.ShapeDtypeStruct((t, QKV_WIDTH), BF16),
                   jax.ShapeDtypeStruct((batch, r1, seq // r1, QKV_WIDTH), BF16),
                   jax.ShapeDtypeStruct((batch, r2, seq // r2, QKV_WIDTH), BF16)],
        scratch_shapes=[pltpu.VMEM((N_SLABS, tm, LANES), F32),
                        pltpu.VMEM((N_SLABS, tm + 2 * HALO, LANES), F32),
                        pltpu.VMEM((N_SLABS, tm, LANES), F32),
                        pltpu.VMEM((N_SLABS, tm, LANES), F32)],
        compiler_params=_params(1),
        name="in_proj",
    )(x2, g, w_in, b_gate, dww, dwb, lng, lnb)


def _attn_kernel(q_ref, k_ref, v_ref, o_ref, lse_ref, bias_ref, *, n_cls, n_sub, length,
                 slope_dist):
    n_blocks = length // Q_SUB

    @pl.when((pl.program_id(0) == 0) & (pl.program_id(1) == 0) & (pl.program_id(2) == 0))
    def _fill_bias():
        row = lax.broadcasted_iota(jnp.int32, (Q_SUB, K_WIN), 0)
        col = lax.broadcasted_iota(jnp.int32, (Q_SUB, K_WIN), 1)
        for e in range(3):
            dist = jnp.abs(col - row - (2 - e) * HALF_BAND)
            distf = dist.astype(F32)
            for h in range(HEADS_PER_GROUP):
                bias_ref[e, h // 2, (h % 2) * Q_SUB:(h % 2 + 1) * Q_SUB, :] = jnp.where(
                    dist <= HALF_BAND, (-LOG2_E * slope_dist[h]) * distf, MASK_VALUE)

    lane = lax.broadcasted_iota(jnp.int32, (1, LANES), 1)
    low_half = lane < HEAD_DIM

    def sub_block(cls, j):
        blk = pl.program_id(2) * n_sub + j
        m0 = blk * Q_SUB
        start = pl.multiple_of(jnp.clip(m0 - HALF_BAND, 0, length - K_WIN), HALF_BAND)
        edge = jnp.where(blk == 0, 2, jnp.where(blk == n_blocks - 1, 0, 1))
        rows = slice(j * Q_SUB, (j + 1) * Q_SUB)
        lse = jnp.zeros((Q_SUB, LANES), F32)
        for pair in range(HEADS_PER_GROUP // 2):
            cols = slice(pair * LANES, (pair + 1) * LANES)
            qs = q_ref[cls, rows, cols]
            ks = k_ref[cls, pl.ds(start, K_WIN), cols]
            vs = v_ref[cls, pl.ds(start, K_WIN), cols]
            zero = jnp.zeros_like(qs)
            q2 = jnp.concatenate([jnp.where(low_half, qs, zero), jnp.where(low_half, zero, qs)], axis=0)
            s = lax.dot_general(q2, ks, (((1,), (1,)), ((), ())), preferred_element_type=F32)
            s = s + bias_ref[edge, pair]
            m = jnp.max(s, axis=-1, keepdims=True)
            p = jnp.exp2(s - m).astype(BF16)
            out = _dot(p, jnp.concatenate([vs, jnp.ones_like(vs)], axis=1))
            num = jnp.where(low_half, out[:Q_SUB, :LANES], out[Q_SUB:, :LANES])
            den = jnp.where(low_half, out[:Q_SUB, LANES:], out[Q_SUB:, LANES:])
            m2 = jnp.where(low_half, m[:Q_SUB], m[Q_SUB:])
            o_ref[cls, rows, cols] = (num / den).astype(BF16)
            lse = jnp.where((lane == pair) | (lane == HEAD_DIM + pair),
                            (m2 + jnp.log2(den)) * LN_2, lse)
        lse_ref[cls, rows, :] = lse

    for cls in range(n_cls):
        for j in range(n_sub):
            sub_block(cls, j)


def _attention_group(qkv, group, queries_per_step):
    batch, r, length, _ = qkv.shape
    bq = min(queries_per_step, length)
    n_cls = queries_per_step // bq
    slope_dist = tuple(float(_SLOPES[group * HEADS_PER_GROUP + h] * r) for h in range(HEADS_PER_GROUP))
    q_spec = pl.BlockSpec((None, n_cls, bq, SLOT_WIDTH), lambda b, c, i: (b, c, i, 0))
    k_spec = pl.BlockSpec((None, n_cls, length, SLOT_WIDTH), lambda b, c, i: (b, c, 0, 1))
    v_spec = pl.BlockSpec((None, n_cls, length, SLOT_WIDTH), lambda b, c, i: (b, c, 0, 2))
    out_spec = pl.BlockSpec((None, n_cls, bq, SLOT_WIDTH), lambda b, c, i: (b, c, i, 0))
    lse_spec = pl.BlockSpec((None, n_cls, bq, LANES), lambda b, c, i: (b, c, i, 0))
    return pl.pallas_call(
        functools.partial(_attn_kernel, n_cls=n_cls, n_sub=bq // Q_SUB, length=length,
                          slope_dist=slope_dist),
        grid=(batch, r // n_cls, length // bq),
        in_specs=[q_spec, k_spec, v_spec],
        out_specs=[out_spec, lse_spec],
        out_shape=[jax.ShapeDtypeStruct((batch, r, length, SLOT_WIDTH), BF16),
                   jax.ShapeDtypeStruct((batch, r, length, LANES), F32)],
        scratch_shapes=[pltpu.VMEM((3, HEADS_PER_GROUP // 2, 2 * Q_SUB, K_WIN), F32)],
        compiler_params=_params(3),
        name=f"attn_g{group}",
    )(qkv, qkv, qkv)


def _interleave(src_ref, dst_ref, r, tm):
    n = tm // r
    n_slabs = dst_ref.shape[0]
    for c in range(r):
        piece = src_ref[c].astype(F32)
        for s in range(n_slabs):
            dst_ref[s, pl.ds(c, n, stride=r), :] = piece[:, s * LANES:(s + 1) * LANES]
    return jnp.concatenate([dst_ref[s] for s in range(n_slabs)], axis=1)


def _head_expansion():
    row = lax.broadcasted_iota(jnp.int32, (LANES, SLOT_WIDTH), 0)
    col = lax.broadcasted_iota(jnp.int32, (LANES, SLOT_WIDTH), 1)
    shift = HEAD_DIM.bit_length() - 1
    pair, odd = row & (HEAD_DIM - 1), lax.shift_right_logical(row, shift)
    hit = (pair < HEADS_PER_GROUP // 2) & (lax.shift_right_logical(col, shift) == 2 * pair + odd)
    e = jnp.where(hit, 1.0, 0.0).astype(BF16)
    return jnp.concatenate([e, e], axis=0)


def _mix_kernel(cact_ref, wco_ref, o0_ref, o1_ref, o2_ref, l0_ref, l1_ref, l2_ref, wao_ref,
                gate_ref, wout_ref, x_ref, gffn_ref, h_ref, un_ref, il_ref, *, tm):
    conv_out = _dot(cact_ref[...], wco_ref[...])

    r1, r2 = ATTN_GROUPS[1][1], ATTN_GROUPS[2][1]
    l0 = l0_ref[...]
    l1 = _interleave(l1_ref, il_ref.at[0:1], r1, tm)
    l2 = _interleave(l2_ref, il_ref.at[1:2], r2, tm)
    mx = jnp.maximum(jnp.maximum(l0, l1), l2)
    e0, e1, e2 = jnp.exp(l0 - mx), jnp.exp(l1 - mx), jnp.exp(l2 - mx)
    inv = 1.0 / (e0 + e1 + e2)
    expansion = _head_expansion()

    def spread(w):
        hi = w.astype(BF16)
        lo = (w - hi.astype(F32)).astype(BF16)
        return _dot(jnp.concatenate([hi, lo], axis=1), expansion)

    o0 = o0_ref[...].astype(F32)
    o1 = _interleave(o1_ref, il_ref.at[2:2 + SLOT_SLABS], r1, tm)
    o2 = _interleave(o2_ref, il_ref.at[2 + SLOT_SLABS:2 + 2 * SLOT_SLABS], r2, tm)
    attn = spread(e0 * inv) * o0 + spread(e1 * inv) * o1 + spread(e2 * inv) * o2
    attn_out = _dot(attn.astype(BF16), wao_ref[...])

    merged = (gate_ref[:, :D_MODEL].astype(F32) * conv_out
              + gate_ref[:, D_MODEL:].astype(F32) * attn_out)
    h = x_ref[...] + _dot(merged.astype(BF16), wout_ref[...])
    h_ref[...] = h
    un_ref[...] = (h * _rms_scale(h) * gffn_ref[...]).astype(BF16)


def _mix(cact, wco, outs, lses, wao, gates, wout, x2, gffn, seq, tm):
    t = x2.shape[0]
    tiles = seq // tm
    row = lambda width: pl.BlockSpec((tm, width), lambda i: (i, 0))

    def class_major(r, width):
        return pl.BlockSpec((None, r, tm // r, width), lambda i: (i // tiles, 0, i % tiles, 0))

    def attn_specs(width):
        return [row(width)] + [class_major(ATTN_GROUPS[g][1], width) for g in (1, 2)]

    flat0 = lambda a: a.reshape(t, a.shape[-1])
    return pl.pallas_call(
        functools.partial(_mix_kernel, tm=tm),
        grid=(t // tm,),
        in_specs=[row(CONV_CH), _resident(wco.shape), *attn_specs(SLOT_WIDTH), *attn_specs(LANES),
                  _resident(wao.shape), row(2 * D_MODEL), _resident(wout.shape), row(D_MODEL),
                  _resident((1, D_MODEL))],
        out_specs=[row(D_MODEL), row(D_MODEL)],
        out_shape=[jax.ShapeDtypeStruct((t, D_MODEL), F32),
                   jax.ShapeDtypeStruct((t, D_MODEL), BF16)],
        scratch_shapes=[pltpu.VMEM((2 + 2 * SLOT_SLABS, tm, LANES), F32)],
        compiler_params=_params(1),
        name="mix",
    )(cact, wco, flat0(outs[0]), outs[1], outs[2], flat0(lses[0]), lses[1], lses[2], wao, gates,
      wout, x2, gffn)


_FF_CHUNK = 256
_FF_SLABS = _FF_CHUNK // LANES
_FF_SLOTS = 3


def _halo_specs(tm, width, n_rows):
    per = tm // HALO
    last = n_rows // HALO - 1
    prev = pl.BlockSpec((HALO, width), lambda i: (jnp.maximum(i * per - 1, 0), 0))
    cur = pl.BlockSpec((tm, width), lambda i: (i, 0))
    nxt = pl.BlockSpec((HALO, width), lambda i: (jnp.minimum((i + 1) * per, last), 0))
    return prev, cur, nxt


def _ffn_kernel(up_ref_prev, uc_ref, un_ref_next, h_ref, wup_ref, dww_ref, dwb_ref, wdn_ref,
                gfin_ref, y_ref, uext_ref, ab_ref, acc_ref, perm_ref, *, tm, tiles_per_seq):
    pos = pl.program_id(0) % tiles_per_seq
    row_id = lax.broadcasted_iota(jnp.int32, (HALO, 1), 0)
    zero = jnp.zeros_like(up_ref_prev)
    uext_ref[0:HALO, :] = jnp.where(
        (row_id == 0) & (pos != tiles_per_seq - 1), un_ref_next[...],
        jnp.where((row_id == HALO - 1) & (pos != 0), up_ref_prev[...], zero))
    uext_ref[HALO:, :] = uc_ref[...]
    uext = uext_ref[...]
    n_chunks = D_FF // _FF_CHUNK
    half_rows = tm // 2

    def up_proj(c):
        for half in range(2):
            col0 = half * D_FF + c * _FF_CHUNK
            res = _dot(uext, wup_ref[:, col0:col0 + _FF_CHUNK])
            for sl in range(_FF_SLABS):
                piece = res[:, sl * LANES:(sl + 1) * LANES]
                ab_ref[c % _FF_SLOTS, half, sl, 0:HALO + tm, :] = piece
                ab_ref[c % _FF_SLOTS, half, sl, HALO + tm:HALO + tm + 1, :] = piece[0:1]

    def conv3_even_odd(src_ref, col0):
        cols = slice(col0, col0 + LANES)
        w0, w1, w2 = (dww_ref[j:j + 1, cols] for j in range(3))
        bias = dwb_ref[:, cols]
        win = [src_ref[pl.ds(HALO - 1 + k, half_rows, stride=2), :] for k in range(4)]
        even = w0 * win[0] + w1 * win[1] + w2 * win[2] + bias
        odd = w0 * win[1] + w1 * win[2] + w2 * win[3] + bias
        return even, odd

    for c in range(_FF_SLOTS - 1):
        up_proj(c)
    for c in range(n_chunks):
        if c + _FF_SLOTS - 1 < n_chunks:
            up_proj(c + _FF_SLOTS - 1)
        ca = c * _FF_CHUNK
        evens, odds = [], []
        for sl in range(_FF_SLABS):
            a_e, a_o = conv3_even_odd(ab_ref.at[c % _FF_SLOTS, 0, sl], ca + sl * LANES)
            v_e, v_o = conv3_even_odd(ab_ref.at[c % _FF_SLOTS, 1, sl], D_FF + ca + sl * LANES)
            evens.append((_silu(a_e) * v_e).astype(BF16))
            odds.append((_silu(a_o) * v_o).astype(BF16))
        s = jnp.concatenate([jnp.concatenate(evens, axis=1), jnp.concatenate(odds, axis=1)], axis=0)
        part = _dot(s, wdn_ref[ca:ca + _FF_CHUNK, :])
        if c == 0:
            acc_ref[...] = part
        else:
            acc_ref[...] += part

    for sl in range(N_SLABS):
        cols = slice(sl * LANES, (sl + 1) * LANES)
        perm_ref[sl, pl.ds(0, half_rows, stride=2), :] = acc_ref[0:half_rows, cols]
        perm_ref[sl, pl.ds(1, half_rows, stride=2), :] = acc_ref[half_rows:tm, cols]
    h = h_ref[...] + jnp.concatenate([perm_ref[sl] for sl in range(N_SLABS)], axis=1)
    y_ref[...] = h * _rms_scale(h) * gfin_ref[...]


def _ffn(un, h, wup, dww, dwb, wdn, gfin, seq, tm):
    t = h.shape[0]
    row = lambda width: pl.BlockSpec((tm, width), lambda i: (i, 0))
    up, uc, unx = _halo_specs(tm, D_MODEL, t)
    return pl.pallas_call(
        functools.partial(_ffn_kernel, tm=tm, tiles_per_seq=seq // tm),
        grid=(t // tm,),
        in_specs=[up, uc, unx, row(D_MODEL), _resident(wup.shape), _resident(dww.shape),
                  _resident(dwb.shape), _resident(wdn.shape), _resident((1, D_MODEL))],
        out_specs=row(D_MODEL),
        out_shape=jax.ShapeDtypeStruct((t, D_MODEL), F32),
        scratch_shapes=[pltpu.VMEM((HALO + tm, D_MODEL), BF16),
                        pltpu.VMEM((_FF_SLOTS, 2, _FF_SLABS, HALO + tm + SUBLANES, LANES), F32),
                        pltpu.VMEM((tm, D_MODEL), F32),
                        pltpu.VMEM((N_SLABS, tm, LANES), F32)],
        compiler_params=_params(1),
        name="ffn",
    )(un, un, un, h, wup, dww, dwb, wdn, gfin)


def kernel(x, norm_mix_g, w_in, b_gate, conv_dw_w, conv_dw_b, conv_ln_g, conv_ln_b, w_conv_out,
           w_attn_out, w_out, norm_ffn_g, w_up, ffn_dw_w, ffn_dw_b, w_down, norm_final_g):
    batch, seq, d = x.shape
    assert w_in.shape[0] == 1, "single-layer block"
    x2 = x.reshape(batch * seq, d)
    cact, gates, qkv0, qkv1, qkv2 = _in_proj(
        x2, norm_mix_g, w_in[0].astype(BF16), b_gate, conv_dw_w[0], conv_dw_b, conv_ln_g,
        conv_ln_b, batch, seq, tm=256)
    qkv0 = qkv0.reshape(batch, 1, seq, QKV_WIDTH)
    outs, lses = zip(*[_attention_group(qkv, g, queries_per_step=2048)
                       for g, qkv in enumerate((qkv0, qkv1, qkv2))])
    h2, un = _mix(cact, w_conv_out[0].astype(BF16), outs, lses, w_attn_out[0].astype(BF16), gates,
                  w_out[0].astype(BF16), x2, norm_ffn_g, seq, tm=512)
    y2 = _ffn(un, h2, w_up[0].astype(BF16), ffn_dw_w[0], ffn_dw_b, w_down[0].astype(BF16),
              norm_final_g[None], seq, tm=512)
    return y2.reshape(batch, seq, d)
```

```python
import functools

import jax
import jax.numpy as jnp
import numpy as np
from jax import lax
from jax.experimental import pallas as pl
from jax.experimental.pallas import tpu as pltpu

F32 = jnp.float32
BF16 = jnp.bfloat16

D_MODEL = 1024
CONV_CH = 1024
CONV_WIDTH = 31
ATTN_GROUPS = ((128, 1), (512, 4), (2048, 16))
N_GROUPS = len(ATTN_GROUPS)
HEADS_PER_GROUP = 8
HEAD_DIM = 64
N_ATTN_HEADS = HEADS_PER_GROUP * N_GROUPS
ATTN_WIDTH = N_ATTN_HEADS * HEAD_DIM
SLOT_WIDTH = HEADS_PER_GROUP * HEAD_DIM
QKV_WIDTH = 3 * SLOT_WIDTH
D_FF = 2816
RMS_EPS = 1e-6
LN_EPS = 1e-5
MASK_VALUE = -1e30
LOG2_E = float(np.log2(np.e))
LN_2 = float(np.log(2.0))

LANES = 128
SUBLANES = 8
N_SLABS = D_MODEL // LANES
SLOT_SLABS = SLOT_WIDTH // LANES
HALF_BAND = 64
Q_SUB = 128
K_WIN = Q_SUB + 2 * HALF_BAND
HALO = 16
CONV_HALF = CONV_WIDTH // 2
V7X_VMEM_BYTES = 64 * 1024 * 1024
VMEM_LIMIT = V7X_VMEM_BYTES * 7 // 8

_OFF_A, _OFF_G, _OFF_Q = 0, CONV_CH, 2 * CONV_CH
_OFF_K, _OFF_V = _OFF_Q + ATTN_WIDTH, _OFF_Q + 2 * ATTN_WIDTH
_OFF_GATE = _OFF_Q + 3 * ATTN_WIDTH
_OFF_END = _OFF_GATE + 2 * D_MODEL

_SLOPES = (np.float32(2.0) ** (np.float32(-8.0) * np.arange(1, N_ATTN_HEADS + 1, dtype=np.float32)
                               / np.float32(N_ATTN_HEADS))).astype(np.float64)


def _sigmoid(x):
    return 0.5 * jnp.tanh(0.5 * x) + 0.5


def _silu(x):
    return x * _sigmoid(x)


def _rms_scale(xf):
    return lax.rsqrt(jnp.mean(xf * xf, axis=-1, keepdims=True) + RMS_EPS)


def _dot(a, b):
    return jnp.dot(a, b, preferred_element_type=F32)


def _resident(shape):
    return pl.BlockSpec(shape, lambda *_: (0,) * len(shape), pipeline_mode=pl.Buffered(1))


def _params(n_axes):
    return pltpu.CompilerParams(dimension_semantics=("arbitrary",) * n_axes,
                                vmem_limit_bytes=VMEM_LIMIT)


_CONV_ROWS = 32
_GLU_UNIT = 256
_PROJ_UNIT = 256


def _in_proj_kernel(x_ref, g_ref, w_ref, bg_ref, dww_ref, dwb_ref, lng_ref, lnb_ref,
                    cact_ref, gate_ref, qkv0_ref, qkv1_ref, qkv2_ref,
                    us_ref, zs_ref, ys_ref, zc_ref, *, tm, tiles_per_seq):
    step = pl.program_id(0)
    starts_seq = step % tiles_per_seq == 0

    @pl.when(step == 0)
    def _init():
        zs_ref[...] = jnp.zeros_like(zs_ref)

    xf = x_ref[...]
    u = xf * _rms_scale(xf) * g_ref[...]
    ub = u.astype(BF16)
    for s in range(N_SLABS):
        us_ref[s] = u[:, s * LANES:(s + 1) * LANES]

    def proj(lhs, lo, hi):
        return _dot(lhs, w_ref[:, lo:hi])

    def glu_unit(k):
        lo = k * _GLU_UNIT
        zk = proj(ub, _OFF_A + lo, _OFF_A + lo + _GLU_UNIT) * _sigmoid(
            proj(ub, _OFF_G + lo, _OFF_G + lo + _GLU_UNIT))
        for j in range(_GLU_UNIT // LANES):
            s = lo // LANES + j
            piece = zk[:, j * LANES:(j + 1) * LANES]
            zc_ref[s] = piece
            zs_ref[s, HALO + tm:, :] = jnp.where(starts_seq, 0.0, piece[0:HALO])

    first = HALO - CONV_HALF
    half_rows = _CONV_ROWS // 2

    def conv_unit(r0, s):
        cols = slice(s * LANES, (s + 1) * LANES)
        tap = lambda j: dww_ref[j:j + 1, cols]
        acc_e = acc_o = jnp.broadcast_to(dwb_ref[:, cols], (half_rows, LANES))
        for k in range(CONV_WIDTH + 1):
            win = zs_ref[s, pl.ds(r0 + first + k, half_rows, stride=2), :]
            if k < CONV_WIDTH:
                acc_e = acc_e + tap(k) * win
            if k > 0:
                acc_o = acc_o + tap(k - 1) * win
        ys_ref[s, pl.ds(r0, half_rows, stride=2), :] = acc_e
        ys_ref[s, pl.ds(r0 + 1, half_rows, stride=2), :] = acc_o

    def norm_unit(r0):
        rows = slice(r0, r0 + _CONV_ROWS)
        y = jnp.concatenate([ys_ref[s, rows, :] for s in range(N_SLABS)], axis=1)
        mu = jnp.mean(y, axis=-1, keepdims=True)
        yc = y - mu
        var = jnp.mean(yc * yc, axis=-1, keepdims=True)
        yn = yc * lax.rsqrt(var + LN_EPS) * lng_ref[...] + lnb_ref[...]
        cact_ref[rows, :] = _silu(yn).astype(BF16)

    def gate_unit(k):
        cols = slice(k * _PROJ_UNIT, (k + 1) * _PROJ_UNIT)
        logits = proj(ub, _OFF_GATE + cols.start, _OFF_GATE + cols.stop) + bg_ref[:, cols]
        gate_ref[:, cols] = _sigmoid(logits).astype(BF16)

    def class_major_lhs(r):
        n = tm // r
        return jnp.concatenate(
            [jnp.concatenate([us_ref[s, pl.ds(c, n, stride=r), :] for s in range(N_SLABS)], axis=1)
             for c in range(r)], axis=0).astype(BF16)

    def qkv_unit(out_ref, lhs, group, which, part):
        off = (_OFF_Q, _OFF_K, _OFF_V)[which] + group * SLOT_WIDTH + part * _PROJ_UNIT
        res = proj(lhs, off, off + _PROJ_UNIT)
        if which == 0:
            res = res * (LOG2_E * HEAD_DIM ** -0.5)
        res = res.astype(BF16)
        cols = slice(which * SLOT_WIDTH + part * _PROJ_UNIT,
                     which * SLOT_WIDTH + (part + 1) * _PROJ_UNIT)
        if group == 0:
            out_ref[:, cols] = res
        else:
            r = ATTN_GROUPS[group][1]
            for c in range(r):
                out_ref[c, :, cols] = res[c * (tm // r):(c + 1) * (tm // r)]

    @pl.when(step >= 0)
    def _interleaved():
        lhs = {0: ub, 1: class_major_lhs(ATTN_GROUPS[1][1]), 2: class_major_lhs(ATTN_GROUPS[2][1])}
        matmul_units = [functools.partial(glu_unit, k) for k in range(CONV_CH // _GLU_UNIT)]
        matmul_units += [functools.partial(gate_unit, k) for k in range(2 * D_MODEL // _PROJ_UNIT)]
        matmul_units += [functools.partial(qkv_unit, out_ref, lhs[group], group, which, part)
                         for group, out_ref in enumerate((qkv0_ref, qkv1_ref, qkv2_ref))
                         for which in range(3) for part in range(SLOT_WIDTH // _PROJ_UNIT)]
        vector_units = []
        for r0 in range(0, tm, _CONV_ROWS):
            vector_units += [functools.partial(conv_unit, r0, s) for s in range(N_SLABS)]
        done = 0
        for i, unit in enumerate(matmul_units):
            unit()
            due = -(-(i + 1) * len(vector_units) // len(matmul_units))
            for vec in vector_units[done:due]:
                vec()
            done = due
        assert done == len(vector_units)

    for r0 in range(0, tm, _CONV_ROWS):
        norm_unit(r0)

    for s in range(N_SLABS):
        tail = zs_ref[s, tm:tm + HALO, :]
        zs_ref[s, 0:HALO, :] = jnp.where(starts_seq, 0.0, tail)
        zs_ref[s, HALO:HALO + tm, :] = zc_ref[s]


def _in_proj(x2, g, w_in, b_gate, dww, dwb, lng, lnb, batch, seq, tm):
    t = x2.shape[0]
    tiles = seq // tm
    n_tiles = t // tm
    cur = lambda i: jnp.minimum(i, n_tiles - 1)
    row = lambda width: pl.BlockSpec((tm, width), lambda i: (cur(i), 0))
    lagged = pl.BlockSpec((tm, CONV_CH), lambda i: (jnp.maximum(i - 1, 0), 0))

    def class_major(r):
        return pl.BlockSpec((None, r, tm // r, QKV_WIDTH),
                            lambda i: (cur(i) // tiles, 0, cur(i) % tiles, 0))

    r1, r2 = ATTN_GROUPS[1][1], ATTN_GROUPS[2][1]
    vec = _resident((1, CONV_CH))
    return pl.pallas_call(
        functools.partial(_in_proj_kernel, tm=tm, tiles_per_seq=tiles),
        grid=(n_tiles + 1,),
        in_specs=[row(D_MODEL), _resident((1, D_MODEL)), _resident(w_in.shape),
                  _resident((1, 2 * D_MODEL)), _resident(dww.shape), vec, vec, vec],
        out_specs=[lagged, row(2 * D_MODEL), row(QKV_WIDTH), class_major(r1), class_major(r2)],
        out_shape=[jax.ShapeDtypeStruct((t, CONV_CH), BF16),
                   jax.ShapeDtypeStruct((t, 2 * D_MODEL), BF16),
                   jax.ShapeDtypeStruct((t, QKV_WIDTH), BF16),
                   jax.ShapeDtypeStruct((batch, r1, seq // r1, QKV_WIDTH), BF16),
                   jax.ShapeDtypeStruct((batch, r2, seq // r2, QKV_WIDTH), BF16)],
        scratch_shapes=[pltpu.VMEM((N_SLABS, tm, LANES), F32),
                        pltpu.VMEM((N_SLABS, tm + 2 * HALO, LANES), F32),
                        pltpu.VMEM((N_SLABS, tm, LANES), F32),
                        pltpu.VMEM((N_SLABS, tm, LANES), F32)],
        compiler_params=_params(1),
        name="in_proj",
    )(x2, g, w_in, b_gate, dww, dwb, lng, lnb)


def _attn_kernel(q_ref, k_ref, v_ref, o_ref, lse_ref, bias_ref, *, n_cls, n_sub, length,
                 slope_dist):
    n_blocks = length // Q_SUB

    @pl.when((pl.program_id(0) == 0) & (pl.program_id(1) == 0) & (pl.program_id(2) == 0))
    def _fill_bias():
        row = lax.broadcasted_iota(jnp.int32, (Q_SUB, K_WIN), 0)
        col = lax.broadcasted_iota(jnp.int32, (Q_SUB, K_WIN), 1)
        for e in range(3):
            dist = jnp.abs(col - row - (2 - e) * HALF_BAND)
            distf = dist.astype(F32)
            for h in range(HEADS_PER_GROUP):
                bias_ref[e, h // 2, (h % 2) * Q_SUB:(h % 2 + 1) * Q_SUB, :] = jnp.where(
                    dist <= HALF_BAND, (-LOG2_E * slope_dist[h]) * distf, MASK_VALUE)

    lane = lax.broadcasted_iota(jnp.int32, (1, LANES), 1)
    low_half = lane < HEAD_DIM

    def sub_block(cls, j):
        blk = pl.program_id(2) * n_sub + j
        m0 = blk * Q_SUB
        start = pl.multiple_of(jnp.clip(m0 - HALF_BAND, 0, length - K_WIN), HALF_BAND)
        edge = jnp.where(blk == 0, 2, jnp.where(blk == n_blocks - 1, 0, 1))
        rows = slice(j * Q_SUB, (j + 1) * Q_SUB)
        lse = jnp.zeros((Q_SUB, LANES), F32)
        for pair in range(HEADS_PER_GROUP // 2):
            cols = slice(pair * LANES, (pair + 1) * LANES)
            qs = q_ref[cls, rows, cols]
            ks = k_ref[cls, pl.ds(start, K_WIN), cols]
            vs = v_ref[cls, pl.ds(start, K_WIN), cols]
            zero = jnp.zeros_like(qs)
            q2 = jnp.concatenate([jnp.where(low_half, qs, zero), jnp.where(low_half, zero, qs)], axis=0)
            s = lax.dot_general(q2, ks, (((1,), (1,)), ((), ())), preferred_element_type=F32)
            s = s + bias_ref[edge, pair]
            m = jnp.max(s, axis=-1, keepdims=True)
            p = jnp.exp2(s - m).astype(BF16)
            out = _dot(p, jnp.concatenate([vs, jnp.ones_like(vs)], axis=1))
            num = jnp.where(low_half, out[:Q_SUB, :LANES], out[Q_SUB:, :LANES])
            den = jnp.where(low_half, out[:Q_SUB, LANES:], out[Q_SUB:, LANES:])
            m2 = jnp.where(low_half, m[:Q_SUB], m[Q_SUB:])
            o_ref[cls, rows, cols] = (num / den).astype(BF16)
            lse = jnp.where((lane == pair) | (lane == HEAD_DIM + pair),
                            (m2 + jnp.log2(den)) * LN_2, lse)
        lse_ref[cls, rows, :] = lse

    for cls in range(n_cls):
        for j in range(n_sub):
            sub_block(cls, j)


def _attention_group(qkv, group, queries_per_step):
    batch, r, length, _ = qkv.shape
    bq = min(queries_per_step, length)
    n_cls = queries_per_step // bq
    slope_dist = tuple(float(_SLOPES[group * HEADS_PER_GROUP + h] * r) for h in range(HEADS_PER_GROUP))
    q_spec = pl.BlockSpec((None, n_cls, bq, SLOT_WIDTH), lambda b, c, i: (b, c, i, 0))
    k_spec = pl.BlockSpec((None, n_cls, length, SLOT_WIDTH), lambda b, c, i: (b, c, 0, 1))
    v_spec = pl.BlockSpec((None, n_cls, length, SLOT_WIDTH), lambda b, c, i: (b, c, 0, 2))
    out_spec = pl.BlockSpec((None, n_cls, bq, SLOT_WIDTH), lambda b, c, i: (b, c, i, 0))
    lse_spec = pl.BlockSpec((None, n_cls, bq, LANES), lambda b, c, i: (b, c, i, 0))
    return pl.pallas_call(
        functools.partial(_attn_kernel, n_cls=n_cls, n_sub=bq // Q_SUB, length=length,
                          slope_dist=slope_dist),
        grid=(batch, r // n_cls, length // bq),
        in_specs=[q_spec, k_spec, v_spec],
        out_specs=[out_spec, lse_spec],
        out_shape=[jax.ShapeDtypeStruct((batch, r, length, SLOT_WIDTH), BF16),
                   jax.ShapeDtypeStruct((batch, r, length, LANES), F32)],
        scratch_shapes=[pltpu.VMEM((3, HEADS_PER_GROUP // 2, 2 * Q_SUB, K_WIN), F32)],
        compiler_params=_params(3),
        name=f"attn_g{group}",
    )(qkv, qkv, qkv)


def _interleave(src_ref, dst_ref, r, tm):
    n = tm // r
    n_slabs = dst_ref.shape[0]
    for c in range(r):
        piece = src_ref[c].astype(F32)
        for s in range(n_slabs):
            dst_ref[s, pl.ds(c, n, stride=r), :] = piece[:, s * LANES:(s + 1) * LANES]
    return jnp.concatenate([dst_ref[s] for s in range(n_slabs)], axis=1)


def _head_expansion():
    row = lax.broadcasted_iota(jnp.int32, (LANES, SLOT_WIDTH), 0)
    col = lax.broadcasted_iota(jnp.int32, (LANES, SLOT_WIDTH), 1)
    shift = HEAD_DIM.bit_length() - 1
    pair, odd = row & (HEAD_DIM - 1), lax.shift_right_logical(row, shift)
    hit = (pair < HEADS_PER_GROUP // 2) & (lax.shift_right_logical(col, shift) == 2 * pair + odd)
    e = jnp.where(hit, 1.0, 0.0).astype(BF16)
    return jnp.concatenate([e, e], axis=0)


def _mix_kernel(cact_ref, wco_ref, o0_ref, o1_ref, o2_ref, l0_ref, l1_ref, l2_ref, wao_ref,
                gate_ref, wout_ref, x_ref, gffn_ref, h_ref, un_ref, il_ref, *, tm):
    conv_out = _dot(cact_ref[...], wco_ref[...])

    r1, r2 = ATTN_GROUPS[1][1], ATTN_GROUPS[2][1]
    l0 = l0_ref[...]
    l1 = _interleave(l1_ref, il_ref.at[0:1], r1, tm)
    l2 = _interleave(l2_ref, il_ref.at[1:2], r2, tm)
    mx = jnp.maximum(jnp.maximum(l0, l1), l2)
    e0, e1, e2 = jnp.exp(l0 - mx), jnp.exp(l1 - mx), jnp.exp(l2 - mx)
    inv = 1.0 / (e0 + e1 + e2)
    expansion = _head_expansion()

    def spread(w):
        hi = w.astype(BF16)
        lo = (w - hi.astype(F32)).astype(BF16)
        return _dot(jnp.concatenate([hi, lo], axis=1), expansion)

    o0 = o0_ref[...].astype(F32)
    o1 = _interleave(o1_ref, il_ref.at[2:2 + SLOT_SLABS], r1, tm)
    o2 = _interleave(o2_ref, il_ref.at[2 + SLOT_SLABS:2 + 2 * SLOT_SLABS], r2, tm)
    attn = spread(e0 * inv) * o0 + spread(e1 * inv) * o1 + spread(e2 * inv) * o2
    attn_out = _dot(attn.astype(BF16), wao_ref[...])

    merged = (gate_ref[:, :D_MODEL].astype(F32) * conv_out
              + gate_ref[:, D_MODEL:].astype(F32) * attn_out)
    h = x_ref[...] + _dot(merged.astype(BF16), wout_ref[...])
    h_ref[...] = h
    un_ref[...] = (h * _rms_scale(h) * gffn_ref[...]).astype(BF16)


def _mix(cact, wco, outs, lses, wao, gates, wout, x2, gffn, seq, tm):
    t = x2.shape[0]
    tiles = seq // tm
    row = lambda width: pl.BlockSpec((tm, width), lambda i: (i, 0))

    def class_major(r, width):
        return pl.BlockSpec((None, r, tm // r, width), lambda i: (i // tiles, 0, i % tiles, 0))

    def attn_specs(width):
        return [row(width)] + [class_major(ATTN_GROUPS[g][1], width) for g in (1, 2)]

    flat0 = lambda a: a.reshape(t, a.shape[-1])
    return pl.pallas_call(
        functools.partial(_mix_kernel, tm=tm),
        grid=(t // tm,),
        in_specs=[row(CONV_CH), _resident(wco.shape), *attn_specs(SLOT_WIDTH), *attn_specs(LANES),
                  _resident(wao.shape), row(2 * D_MODEL), _resident(wout.shape), row(D_MODEL),
                  _resident((1, D_MODEL))],
        out_specs=[row(D_MODEL), row(D_MODEL)],
        out_shape=[jax.ShapeDtypeStruct((t, D_MODEL), F32),
                   jax.ShapeDtypeStruct((t, D_MODEL), BF16)],
        scratch_shapes=[pltpu.VMEM((2 + 2 * SLOT_SLABS, tm, LANES), F32)],
        compiler_params=_params(1),
        name="mix",
    )(cact, wco, flat0(outs[0]), outs[1], outs[2], flat0(lses[0]), lses[1], lses[2], wao, gates,
      wout, x2, gffn)


_FF_CHUNK = 256
_FF_SLABS = _FF_CHUNK // LANES
_FF_SLOTS = 3


def _halo_specs(tm, width, n_rows):
    per = tm // HALO
    last = n_rows // HALO - 1
    prev = pl.BlockSpec((HALO, width), lambda i: (jnp.maximum(i * per - 1, 0), 0))
    cur = pl.BlockSpec((tm, width), lambda i: (i, 0))
    nxt = pl.BlockSpec((HALO, width), lambda i: (jnp.minimum((i + 1) * per, last), 0))
    return prev, cur, nxt


def _ffn_kernel(up_ref_prev, uc_ref, un_ref_next, h_ref, wup_ref, dww_ref, dwb_ref, wdn_ref,
                gfin_ref, y_ref, uext_ref, ab_ref, acc_ref, perm_ref, *, tm, tiles_per_seq):
    pos = pl.program_id(0) % tiles_per_seq
    row_id = lax.broadcasted_iota(jnp.int32, (HALO, 1), 0)
    zero = jnp.zeros_like(up_ref_prev)
    uext_ref[0:HALO, :] = jnp.where(
        (row_id == 0) & (pos != tiles_per_seq - 1), un_ref_next[...],
        jnp.where((row_id == HALO - 1) & (pos != 0), up_ref_prev[...], zero))
    uext_ref[HALO:, :] = uc_ref[...]
    uext = uext_ref[...]
    n_chunks = D_FF // _FF_CHUNK
    half_rows = tm // 2

    def up_proj(c):
        for half in range(2):
            col0 = half * D_FF + c * _FF_CHUNK
            res = _dot(uext, wup_ref[:, col0:col0 + _FF_CHUNK])
            for sl in range(_FF_SLABS):
                piece = res[:, sl * LANES:(sl + 1) * LANES]
                ab_ref[c % _FF_SLOTS, half, sl, 0:HALO + tm, :] = piece
                ab_ref[c % _FF_SLOTS, half, sl, HALO + tm:HALO + tm + 1, :] = piece[0:1]

    def conv3_even_odd(src_ref, col0):
        cols = slice(col0, col0 + LANES)
        w0, w1, w2 = (dww_ref[j:j + 1, cols] for j in range(3))
        bias = dwb_ref[:, cols]
        win = [src_ref[pl.ds(HALO - 1 + k, half_rows, stride=2), :] for k in range(4)]
        even = w0 * win[0] + w1 * win[1] + w2 * win[2] + bias
        odd = w0 * win[1] + w1 * win[2] + w2 * win[3] + bias
        return even, odd

    for c in range(_FF_SLOTS - 1):
        up_proj(c)
    for c in range(n_chunks):
        if c + _FF_SLOTS - 1 < n_chunks:
            up_proj(c + _FF_SLOTS - 1)
        ca = c * _FF_CHUNK
        evens, odds = [], []
        for sl in range(_FF_SLABS):
            a_e, a_o = conv3_even_odd(ab_ref.at[c % _FF_SLOTS, 0, sl], ca + sl * LANES)
            v_e, v_o = conv3_even_odd(ab_ref.at[c % _FF_SLOTS, 1, sl], D_FF + ca + sl * LANES)
            evens.append((_silu(a_e) * v_e).astype(BF16))
            odds.append((_silu(a_o) * v_o).astype(BF16))
        s = jnp.concatenate([jnp.concatenate(evens, axis=1), jnp.concatenate(odds, axis=1)], axis=0)
        part = _dot(s, wdn_ref[ca:ca + _FF_CHUNK, :])
        if c == 0:
            acc_ref[...] = part
        else:
            acc_ref[...] += part

    for sl in range(N_SLABS):
        cols = slice(sl * LANES, (sl + 1) * LANES)
        perm_ref[sl, pl.ds(0, half_rows, stride=2), :] = acc_ref[0:half_rows, cols]
        perm_ref[sl, pl.ds(1, half_rows, stride=2), :] = acc_ref[half_rows:tm, cols]
    h = h_ref[...] + jnp.concatenate([perm_ref[sl] for sl in range(N_SLABS)], axis=1)
    y_ref[...] = h * _rms_scale(h) * gfin_ref[...]


def _ffn(un, h, wup, dww, dwb, wdn, gfin, seq, tm):
    t = h.shape[0]
    row = lambda width: pl.BlockSpec((tm, width), lambda i: (i, 0))
    up, uc, unx = _halo_specs(tm, D_MODEL, t)
    return pl.pallas_call(
        functools.partial(_ffn_kernel, tm=tm, tiles_per_seq=seq // tm),
        grid=(t // tm,),
        in_specs=[up, uc, unx, row(D_MODEL), _resident(wup.shape), _resident(dww.shape),
                  _resident(dwb.shape), _resident(wdn.shape), _resident((1, D_MODEL))],
        out_specs=row(D_MODEL),
        out_shape=jax.ShapeDtypeStruct((t, D_MODEL), F32),
        scratch_shapes=[pltpu.VMEM((HALO + tm, D_MODEL), BF16),
                        pltpu.VMEM((_FF_SLOTS, 2, _FF_SLABS, HALO + tm + SUBLANES, LANES), F32),
                        pltpu.VMEM((tm, D_MODEL), F32),
                        pltpu.VMEM((N_SLABS, tm, LANES), F32)],
        compiler_params=_params(1),
        name="ffn",
    )(un, un, un, h, wup, dww, dwb, wdn, gfin)


def kernel(x, norm_mix_g, w_in, b_gate, conv_dw_w, conv_dw_b, conv_ln_g, conv_ln_b, w_conv_out,
           w_attn_out, w_out, norm_ffn_g, w_up, ffn_dw_w, ffn_dw_b, w_down, norm_final_g):
    batch, seq, d = x.shape
    assert w_in.shape[0] == 1, "single-layer block"
    x2 = x.reshape(batch * seq, d)
    cact, gates, qkv0, qkv1, qkv2 = _in_proj(
        x2, norm_mix_g, w_in[0].astype(BF16), b_gate, conv_dw_w[0], conv_dw_b, conv_ln_g,
        conv_ln_b, batch, seq, tm=256)
    qkv0 = qkv0.reshape(batch, 1, seq, QKV_WIDTH)
    outs, lses = zip(*[_attention_group(qkv, g, queries_per_step=4096)
                       for g, qkv in enumerate((qkv0, qkv1, qkv2))])
    h2, un = _mix(cact, w_conv_out[0].astype(BF16), outs, lses, w_attn_out[0].astype(BF16), gates,
                  w_out[0].astype(BF16), x2, norm_ffn_g, seq, tm=512)
    y2 = _ffn(un, h2, w_up[0].astype(BF16), ffn_dw_w[0], ffn_dw_b, w_down[0].astype(BF16),
              norm_final_g[None], seq, tm=512)
    return y2.reshape(batch, seq, d)
```
